```python
import math
import jax
import jax.numpy as jnp
from jax import lax
import numpy as np


D_MODEL = 2048
BATCH = 2
SEQ = 4096
DEPTH = 4

N_EVEN = (DEPTH + 1) // 2
N_ODD = DEPTH // 2
EPS = 1e-6

S5_WIDTH = D_MODEL // 2
S5_GROUP_SIZE = 16
S5_GROUPS = S5_WIDTH // S5_GROUP_SIZE
S5_STATE = 64
S5_MIN_DECAY = 1e-4

HGRN_WIDTH = D_MODEL - S5_WIDTH
HGRN_HEAD_DIM = 128
HGRN_HEADS = HGRN_WIDTH // HGRN_HEAD_DIM
HGRN_CHUNK = 64

IN_EVEN = S5_WIDTH + 4 * HGRN_WIDTH

ATT_HEAD_DIM = 64
ATT_HEADS = D_MODEL // ATT_HEAD_DIM
ATT_GROUP = 8
ATT_KV_HEADS = ATT_HEADS // ATT_GROUP
WINDOW = 128
ATT_BLOCK = 128
QKV_WIDTH = (ATT_HEADS + 2 * ATT_KV_HEADS) * ATT_HEAD_DIM

D_FF = 4 * D_MODEL

kernel_name = 'hybrid_s5_hgrn2_swa_block'

F32 = jnp.float32


def rms_norm(x, gain):
    xf = x.astype(F32)
    y = xf * lax.rsqrt(jnp.mean(xf * xf, axis=-1, keepdims=True) + EPS)
    return (y * gain.astype(F32)).astype(x.dtype)


def alibi_slopes(n_heads):
    return jnp.exp2(-8.0 * jnp.arange(1, n_heads + 1, dtype=F32) / n_heads)


def hgrn_lower_bounds(lb_param):
    p = jax.nn.softmax(lb_param.astype(F32), axis=0)
    return jnp.cumsum(p, axis=0) - p[0:1]


def s5_mixer(u, lam_re, lam_im, log_dt, b_re, b_im, c_re, c_im, d_skip, w_glu, b_glu):
    bsz, seqlen, _ = u.shape
    uf = u.astype(F32).reshape(bsz, seqlen, S5_GROUPS, S5_GROUP_SIZE)
    lr = jnp.minimum(lam_re.astype(F32), -S5_MIN_DECAY)
    li = lam_im.astype(F32)
    dt = jnp.exp(log_dt.astype(F32))[:, None]
    mag = jnp.exp(lr * dt)
    ar = mag * jnp.cos(li * dt)
    ai = mag * jnp.sin(li * dt)
    den = lr * lr + li * li
    zr = ((ar - 1.0) * lr + ai * li) / den
    zi = (ai * lr - (ar - 1.0) * li) / den
    br, bi = b_re.astype(F32), b_im.astype(F32)
    bbr = zr[..., None] * br - zi[..., None] * bi
    bbi = zr[..., None] * bi + zi[..., None] * br
    bu_re = jnp.einsum('blgc,gpc->blgp', uf, bbr)
    bu_im = jnp.einsum('blgc,gpc->blgp', uf, bbi)
    a_re = jnp.broadcast_to(ar, bu_re.shape)
    a_im = jnp.broadcast_to(ai, bu_im.shape)

    def combine(e1, e2):
        a1r, a1i, b1r, b1i = e1
        a2r, a2i, b2r, b2i = e2
        return (a2r * a1r - a2i * a1i,
                a2r * a1i + a2i * a1r,
                a2r * b1r - a2i * b1i + b2r,
                a2r * b1i + a2i * b1r + b2i)

    _, _, x_re, x_im = lax.associative_scan(combine, (a_re, a_im, bu_re, bu_im), axis=1)
    y = (jnp.einsum('blgp,gcp->blgc', x_re, c_re.astype(F32))
         - jnp.einsum('blgp,gcp->blgc', x_im, c_im.astype(F32))
         + d_skip.astype(F32) * uf)
    y = jax.nn.gelu(y.reshape(bsz, seqlen, S5_WIDTH))
    return y * jax.nn.sigmoid(y @ w_glu.astype(F32) + b_glu.astype(F32))


def chunkwise_gated_recurrence(q, log_f, k, v):
    bsz, seqlen, heads, dk = q.shape
    dv = v.shape[-1]
    n_chunks = seqlen // HGRN_CHUNK

    def to_chunks(t):
        return t.reshape(bsz, n_chunks, HGRN_CHUNK, heads, t.shape[-1]).transpose(1, 0, 3, 2, 4)

    qc, gc, kc, vc = to_chunks(q), to_chunks(log_f), to_chunks(k), to_chunks(v)
    causal = jnp.tril(jnp.ones((HGRN_CHUNK, HGRN_CHUNK), dtype=bool))[:, :, None]

    def step(state, inputs):
        q_, g_, k_, v_ = inputs
        b = jnp.cumsum(g_, axis=2)
        diff = b[:, :, :, None, :] - b[:, :, None, :, :]
        decay = jnp.exp(jnp.where(causal, diff, -jnp.inf))
        scores = jnp.einsum('bhtk,bhsk,bhtsk->bhts', q_, k_, decay)
        out = (jnp.einsum('bhts,bhsv->bhtv', scores, v_)
               + jnp.einsum('bhtk,bhkv->bhtv', q_ * jnp.exp(b), state))
        b_last = b[:, :, -1:, :]
        state = (jnp.exp(b_last[:, :, 0, :])[..., None] * state
                 + jnp.einsum('bhsk,bhsv->bhkv', k_ * jnp.exp(b_last - b), v_))
        return state, out

    state0 = jnp.zeros((bsz, heads, dk, dv), F32)
    _, outs = lax.scan(step, state0, (qc, gc, kc, vc))
    return outs.transpose(1, 0, 3, 2, 4).reshape(bsz, seqlen, heads, dv)


def hgrn2_mixer(q, f, i, g, lower_bound, o_gain):
    bsz, seqlen, _ = q.shape

    def heads(t):
        return t.astype(F32).reshape(bsz, seqlen, HGRN_HEADS, HGRN_HEAD_DIM)

    lb = lower_bound.reshape(HGRN_HEADS, HGRN_HEAD_DIM)
    forget = lb + (1.0 - lb) * jax.nn.sigmoid(heads(f))
    o = chunkwise_gated_recurrence(jax.nn.silu(heads(q)), jnp.log(forget), 1.0 - forget, heads(i))
    o = rms_norm(o, o_gain) * jax.nn.silu(heads(g))
    return o.reshape(bsz, seqlen, HGRN_WIDTH)


def s5_hgrn2_layer(xn, w_in, lam_re, lam_im, log_dt, b_re, b_im, c_re, c_im, d_skip,
                   w_glu, b_glu, lower_bound, o_gain, w_out):
    proj = xn @ w_in
    u, q, f, i, g = jnp.split(proj, [S5_WIDTH, S5_WIDTH + HGRN_WIDTH,
                                     S5_WIDTH + 2 * HGRN_WIDTH, S5_WIDTH + 3 * HGRN_WIDTH], axis=-1)
    y_a = s5_mixer(u, lam_re, lam_im, log_dt, b_re, b_im, c_re, c_im, d_skip, w_glu, b_glu)
    y_b = hgrn2_mixer(q, f, i, g, lower_bound, o_gain)
    return jnp.concatenate([y_a, y_b], axis=-1).astype(xn.dtype) @ w_out


def sliding_window_attention(q, k, v, sinks):
    bsz, seqlen = q.shape[:2]
    nb = seqlen // ATT_BLOCK
    qb = q.reshape(bsz, nb, ATT_BLOCK, ATT_KV_HEADS, ATT_GROUP, ATT_HEAD_DIM)

    def band(t):
        tb = t.reshape(bsz, nb, ATT_BLOCK, ATT_KV_HEADS, ATT_HEAD_DIM)
        prev = jnp.pad(tb, ((0, 0), (1, 0), (0, 0), (0, 0), (0, 0)))[:, :-1]
        return jnp.concatenate([prev, tb], axis=2)

    kw, vw = band(k), band(v)
    scores = jnp.einsum('bnqkgd,bnskd->bnkgqs', qb, kw) * (ATT_HEAD_DIM ** -0.5)
    dist = jnp.arange(ATT_BLOCK)[:, None] + ATT_BLOCK - jnp.arange(2 * ATT_BLOCK)[None, :]
    in_window = (dist >= 0) & (dist < WINDOW)
    key_pos = (jnp.arange(nb)[:, None] * ATT_BLOCK - ATT_BLOCK
               + jnp.arange(2 * ATT_BLOCK)[None, :])
    mask = in_window[None] & (key_pos >= 0)[:, None, :]
    slopes = alibi_slopes(ATT_HEADS).reshape(ATT_KV_HEADS, ATT_GROUP)
    bias = -slopes[:, :, None, None] * dist.astype(F32)
    scores = jnp.where(mask[None, :, None, None], scores + bias, -jnp.inf)
    sink = jnp.broadcast_to(sinks.astype(F32).reshape(ATT_KV_HEADS, ATT_GROUP)[None, None, :, :, None, None],
                            scores.shape[:-1] + (1,))
    probs = jax.nn.softmax(jnp.concatenate([scores, sink], axis=-1), axis=-1)[..., :-1]
    out = jnp.einsum('bnkgqs,bnskd->bnqkgd', probs, vw)
    return out.reshape(bsz, seqlen, ATT_HEADS * ATT_HEAD_DIM)


def attention_layer(xn, w_qkv, q_gain, k_gain, sinks, w_out):
    bsz, seqlen, _ = xn.shape
    proj = xn @ w_qkv
    q, k, v = jnp.split(proj, [ATT_HEADS * ATT_HEAD_DIM, (ATT_HEADS + ATT_KV_HEADS) * ATT_HEAD_DIM], axis=-1)
    q = rms_norm(q.astype(F32).reshape(bsz, seqlen, ATT_HEADS, ATT_HEAD_DIM), q_gain)
    k = rms_norm(k.astype(F32).reshape(bsz, seqlen, ATT_KV_HEADS, ATT_HEAD_DIM), k_gain)
    v = v.astype(F32).reshape(bsz, seqlen, ATT_KV_HEADS, ATT_HEAD_DIM)
    o = sliding_window_attention(q, k, v, sinks)
    return o.astype(xn.dtype) @ w_out


def squared_relu_mlp(xn, w_up, w_down):
    return jnp.square(jax.nn.relu(xn @ w_up)) @ w_down


def setup_inputs(seed: int = 0) -> dict:
    key = jax.random.key(seed)
    ks = iter(jax.random.split(key, 32))

    def nrm(shape, scale):
        return scale * jax.random.normal(next(ks), shape, F32)

    x = nrm((BATCH, SEQ, D_MODEL), 1.0)
    even_norm = 1.0 + nrm((N_EVEN, D_MODEL), 0.02)
    even_w_in = nrm((N_EVEN, D_MODEL, IN_EVEN), D_MODEL ** -0.5)
    s5_lambda_re = -0.5 + nrm((N_EVEN, S5_GROUPS, S5_STATE), 0.01)
    s5_lambda_im = math.pi * jnp.arange(S5_STATE, dtype=F32) + nrm((N_EVEN, S5_GROUPS, S5_STATE), 0.01)
    s5_log_dt = jax.random.uniform(next(ks), (N_EVEN, S5_GROUPS), F32, math.log(1e-3), math.log(1e-1))
    s5_b_re = nrm((N_EVEN, S5_GROUPS, S5_STATE, S5_GROUP_SIZE), (2 * S5_GROUP_SIZE) ** -0.5)
    s5_b_im = nrm((N_EVEN, S5_GROUPS, S5_STATE, S5_GROUP_SIZE), (2 * S5_GROUP_SIZE) ** -0.5)
    s5_c_re = nrm((N_EVEN, S5_GROUPS, S5_GROUP_SIZE, S5_STATE), S5_STATE ** -0.5)
    s5_c_im = nrm((N_EVEN, S5_GROUPS, S5_GROUP_SIZE, S5_STATE), S5_STATE ** -0.5)
    s5_d = nrm((N_EVEN, S5_GROUPS, S5_GROUP_SIZE), 1.0)
    s5_w_glu = nrm((N_EVEN, S5_WIDTH, S5_WIDTH), S5_WIDTH ** -0.5)
    s5_b_glu = nrm((N_EVEN, S5_WIDTH), 0.01)
    hgrn_lower_bound = nrm((N_EVEN, HGRN_WIDTH), 0.1)
    hgrn_o_norm = 1.0 + nrm((N_EVEN, HGRN_HEAD_DIM), 0.02)
    even_w_out = nrm((N_EVEN, D_MODEL, D_MODEL), D_MODEL ** -0.5)
    odd_norm = 1.0 + nrm((N_ODD, D_MODEL), 0.02)
    odd_w_qkv = nrm((N_ODD, D_MODEL, QKV_WIDTH), D_MODEL ** -0.5)
    q_norm = 1.0 + nrm((N_ODD, ATT_HEAD_DIM), 0.02)
    k_norm = 1.0 + nrm((N_ODD, ATT_HEAD_DIM), 0.02)
    att_sinks = nrm((N_ODD, ATT_HEADS), 0.5)
    odd_w_out = nrm((N_ODD, D_MODEL, D_MODEL), D_MODEL ** -0.5)
    mlp_norm = 1.0 + nrm((DEPTH, D_MODEL), 0.02)
    mlp_w_up = nrm((DEPTH, D_MODEL, D_FF), D_MODEL ** -0.5)
    mlp_w_down = nrm((DEPTH, D_FF, D_MODEL), D_FF ** -0.5)
    return {'x': x, 'even_norm': even_norm, 'even_w_in': even_w_in,
            's5_lambda_re': s5_lambda_re, 's5_lambda_im': s5_lambda_im, 's5_log_dt': s5_log_dt,
            's5_b_re': s5_b_re, 's5_b_im': s5_b_im, 's5_c_re': s5_c_re, 's5_c_im': s5_c_im,
            's5_d': s5_d, 's5_w_glu': s5_w_glu, 's5_b_glu': s5_b_glu,
            'hgrn_lower_bound': hgrn_lower_bound, 'hgrn_o_norm': hgrn_o_norm, 'even_w_out': even_w_out,
            'odd_norm': odd_norm, 'odd_w_qkv': odd_w_qkv, 'q_norm': q_norm, 'k_norm': k_norm,
            'att_sinks': att_sinks, 'odd_w_out': odd_w_out,
            'mlp_norm': mlp_norm, 'mlp_w_up': mlp_w_up, 'mlp_w_down': mlp_w_down}


def reference(x, even_norm, even_w_in, s5_lambda_re, s5_lambda_im, s5_log_dt, s5_b_re, s5_b_im,
              s5_c_re, s5_c_im, s5_d, s5_w_glu, s5_b_glu, hgrn_lower_bound, hgrn_o_norm, even_w_out,
              odd_norm, odd_w_qkv, q_norm, k_norm, att_sinks, odd_w_out, mlp_norm, mlp_w_up, mlp_w_down):
    h = x
    lower_bounds = hgrn_lower_bounds(hgrn_lower_bound)
    for layer in range(DEPTH):
        j = layer // 2
        if layer % 2 == 0:
            y = s5_hgrn2_layer(rms_norm(h, even_norm[j]), even_w_in[j], s5_lambda_re[j], s5_lambda_im[j],
                               s5_log_dt[j], s5_b_re[j], s5_b_im[j], s5_c_re[j], s5_c_im[j], s5_d[j],
                               s5_w_glu[j], s5_b_glu[j], lower_bounds[j], hgrn_o_norm[j], even_w_out[j])
        else:
            y = attention_layer(rms_norm(h, odd_norm[j]), odd_w_qkv[j], q_norm[j], k_norm[j],
                                att_sinks[j], odd_w_out[j])
        h = h + y.astype(h.dtype)
        h = h + squared_relu_mlp(rms_norm(h, mlp_norm[layer]), mlp_w_up[layer], mlp_w_down[layer]).astype(h.dtype)
    return h
```

```python
import functools
import math

import jax
import jax.numpy as jnp
from jax import lax
from jax.experimental import pallas as pl
from jax.experimental.pallas import tpu as pltpu

F32 = jnp.float32
BF16 = jnp.bfloat16
EPS = 1e-6

LANES = 128
SUBLANES = 8
VMEM_LIMIT = 56 * 1024 * 1024

S5_GROUP_SIZE = 16
S5_STATE = 64
S5_MIN_DECAY = 1e-4
HGRN_HEAD_DIM = 128
HGRN_CHUNK = 64
ATT_HEAD_DIM = 64
ATT_GROUP = 8
WINDOW = 128
ATT_BLOCK = 128

ROW_TILE = 1024
NORM_ROWS = 128
S5_ROWS = 256
S5_SEGS = SUBLANES
S5_SEG = S5_ROWS // S5_SEGS
S5_PITCH = S5_SEG + 8
S5_TILE_GROUPS = LANES // S5_GROUP_SIZE
S5_TILE_STATES = S5_TILE_GROUPS * S5_STATE
HGRN_ROWS = 512


def _cparams(*sem):
    return pltpu.CompilerParams(dimension_semantics=sem, vmem_limit_bytes=VMEM_LIMIT)


def _split_bf16(x):
    hi = x.astype(BF16)
    lo = (x - hi.astype(F32)).astype(BF16)
    return hi, lo


def _rmsnorm_rows(h_ref, g_ref, xn_ref):
    rows = h_ref.shape[0]
    g = g_ref[...]

    def body(i, carry):
        r = pl.multiple_of(i * NORM_ROWS, NORM_ROWS)
        x = h_ref[pl.ds(r, NORM_ROWS), :]
        ms = jnp.mean(x * x, axis=-1, keepdims=True)
        xn_ref[pl.ds(r, NORM_ROWS), :] = (x * lax.rsqrt(ms + EPS) * g).astype(BF16)
        return carry

    lax.fori_loop(0, rows // NORM_ROWS, body, 0)


def _norm_matmul_kernel(h_ref, g_ref, w_ref, o_ref, xn_ref):
    @pl.when(pl.program_id(1) == 0)
    def _():
        _rmsnorm_rows(h_ref, g_ref, xn_ref)

    o_ref[...] = jnp.dot(xn_ref[...], w_ref[...].astype(BF16), preferred_element_type=F32)


def norm_matmul(h, gain, w, tn):
    m, k = h.shape
    n = w.shape[1]
    tm = min(ROW_TILE, m)
    return pl.pallas_call(
        _norm_matmul_kernel,
        grid=(m // tm, n // tn),
        in_specs=[
            pl.BlockSpec((tm, k), lambda i, j: (i, 0)),
            pl.BlockSpec((1, k), lambda i, j: (0, 0)),
            pl.BlockSpec((k, tn), lambda i, j: (0, j)),
        ],
        out_specs=pl.BlockSpec((tm, tn), lambda i, j: (i, j)),
        out_shape=jax.ShapeDtypeStruct((m, n), F32),
        scratch_shapes=[pltpu.VMEM((tm, k), BF16)],
        compiler_params=_cparams("parallel", "arbitrary"),
        name="norm_matmul",
    )(h, gain.reshape(1, k), w)


def _proj_residual_kernel(*refs, n_lhs):
    lhs_refs = refs[:n_lhs]
    w_ref, res_ref, o_ref = refs[n_lhs:]
    acc = res_ref[...]
    k0 = 0
    for lhs_ref in lhs_refs:
        kk = lhs_ref.shape[1]
        acc = acc + jnp.dot(lhs_ref[...], w_ref[k0:k0 + kk, :].astype(BF16),
                            preferred_element_type=F32)
        k0 += kk
    o_ref[...] = acc


def proj_residual(lhs_list, w, res, tn):
    m, n = res.shape
    k = w.shape[0]
    tm = min(ROW_TILE, m)
    in_specs = [pl.BlockSpec((tm, x.shape[1]), lambda i, j: (i, 0)) for x in lhs_list]
    in_specs += [pl.BlockSpec((k, tn), lambda i, j: (0, j)),
                 pl.BlockSpec((tm, tn), lambda i, j: (i, j))]
    return pl.pallas_call(
        functools.partial(_proj_residual_kernel, n_lhs=len(lhs_list)),
        grid=(m // tm, n // tn),
        in_specs=in_specs,
        out_specs=pl.BlockSpec((tm, tn), lambda i, j: (i, j)),
        out_shape=jax.ShapeDtypeStruct((m, n), F32),
        compiler_params=_cparams("parallel", "arbitrary"),
        name="proj_residual",
    )(*lhs_list, w, res)


def _mlp_kernel(h_ref, g_ref, wu_ref, wd_ref, o_ref, xn_ref):
    @pl.when(pl.program_id(1) == 0)
    def _():
        _rmsnorm_rows(h_ref, g_ref, xn_ref)
        o_ref[...] = h_ref[...]

    up = jnp.dot(xn_ref[...], wu_ref[...].astype(BF16), preferred_element_type=F32)
    act = jnp.square(jnp.maximum(up, 0.0)).astype(BF16)
    o_ref[...] += jnp.dot(act, wd_ref[...].astype(BF16), preferred_element_type=F32)


def mlp_residual(h, gain, w_up, w_down, tf):
    m, d = h.shape
    f = w_up.shape[1]
    tm = min(ROW_TILE, m)
    return pl.pallas_call(
        _mlp_kernel,
        grid=(m // tm, f // tf),
        in_specs=[
            pl.BlockSpec((tm, d), lambda i, j: (i, 0)),
            pl.BlockSpec((1, d), lambda i, j: (0, 0)),
            pl.BlockSpec((d, tf), lambda i, j: (0, j)),
            pl.BlockSpec((tf, d), lambda i, j: (j, 0)),
        ],
        out_specs=pl.BlockSpec((tm, d), lambda i, j: (i, 0)),
        out_shape=jax.ShapeDtypeStruct((m, d), F32),
        scratch_shapes=[pltpu.VMEM((tm, d), BF16)],
        compiler_params=_cparams("parallel", "arbitrary"),
        name="mlp_residual",
    )(h, gain.reshape(1, d), w_up, w_down)


def _s5_prep_kernel(lre_ref, lim_ref, ldt_ref, bre_ref, bim_ref,
                    ar_ref, ai_ref, asr_ref, asi_ref, bbr_ref, bbi_ref):
    lr = jnp.minimum(lre_ref[...], -S5_MIN_DECAY)
    li = lim_ref[...]
    dt = jnp.exp(ldt_ref[...])
    mag = jnp.exp(lr * dt)
    ar = mag * jnp.cos(li * dt)
    ai = mag * jnp.sin(li * dt)
    den = lr * lr + li * li
    zr = ((ar - 1.0) * lr + ai * li) / den
    zi = (ai * lr - (ar - 1.0) * li) / den
    ar_ref[...] = ar
    ai_ref[...] = ai
    pr, pi = ar, ai
    for _ in range(int(math.log2(S5_SEG))):
        pr, pi = pr * pr - pi * pi, 2.0 * pr * pi
    asr_ref[...] = pr
    asi_ref[...] = pi
    for c in range(S5_GROUP_SIZE):
        br = bre_ref[c]
        bi = bim_ref[c]
        bbr_ref[c] = zr * br - zi * bi
        bbi_ref[c] = zr * bi + zi * br


def s5_prep(lam_re, lam_im, log_dt, b_re, b_im):
    g, p = lam_re.shape
    ldt = jnp.broadcast_to(log_dt[:, None], (g, p))
    bre_t = jnp.transpose(b_re, (2, 0, 1))
    bim_t = jnp.transpose(b_im, (2, 0, 1))
    gp = jax.ShapeDtypeStruct((g, p), F32)
    cgp = jax.ShapeDtypeStruct((S5_GROUP_SIZE, g, p), F32)
    return pl.pallas_call(
        _s5_prep_kernel,
        out_shape=(gp, gp, gp, gp, cgp, cgp),
        name="s5_prep",
    )(lam_re, lam_im, ldt, bre_t, bim_t)


def _s5_kernel(u_ref, bblk_ref, ca_ref, cb_ref, ar_ref, ai_ref, asr_ref, asi_ref,
               d_ref, wglu_ref, bglu_ref, o_ref,
               bu_ref, xp_ref, yp_ref, yn_ref, st_ref):
    n_tiles = u_ref.shape[1] // LANES
    half = S5_TILE_STATES // LANES
    slabs = 2 * half

    @pl.when(pl.program_id(1) == 0)
    def _():
        st_ref[...] = jnp.zeros_like(st_ref)

    u = u_ref[...]
    ub = u.astype(BF16)
    for j in range(n_tiles):
        res = jnp.dot(ub[:, j * LANES:(j + 1) * LANES], bblk_ref[j], preferred_element_type=F32)
        for seg in range(S5_SEGS):
            for k in range(slabs):
                bu_ref[j * slabs + k, seg * S5_PITCH:seg * S5_PITCH + S5_SEG, :] = (
                    res[seg * S5_SEG:(seg + 1) * S5_SEG, k * LANES:(k + 1) * LANES])

    row = lax.broadcasted_iota(jnp.int32, (SUBLANES, LANES), 0)

    def tile_scan(j, carry):
        a_r = [ar_ref[j, :, k * LANES:(k + 1) * LANES] for k in range(half)]
        a_i = [ai_ref[j, :, k * LANES:(k + 1) * LANES] for k in range(half)]

        def load(s, k):
            return bu_ref[j * slabs + k, pl.ds(s, S5_SEGS, stride=S5_PITCH), :]

        def step1(s, x):
            xr, xi = x
            nr, ni = [], []
            for k in range(half):
                br = load(s, k)
                bi = load(s, half + k)
                nr.append(a_r[k] * xr[k] - a_i[k] * xi[k] + br)
                ni.append(a_r[k] * xi[k] + a_i[k] * xr[k] + bi)
            return tuple(nr), tuple(ni)

        zero = tuple(jnp.zeros((SUBLANES, LANES), F32) for _ in range(half))
        fr, fi = lax.fori_loop(0, S5_SEG, step1, (zero, zero), unroll=4)

        init_r, init_i, new_r, new_i = [], [], [], []
        for k in range(half):
            sr = asr_ref[j, :, k * LANES:(k + 1) * LANES]
            si = asi_ref[j, :, k * LANES:(k + 1) * LANES]
            pr = st_ref[j, :, k * LANES:(k + 1) * LANES]
            pi = st_ref[j, :, (half + k) * LANES:(half + k + 1) * LANES]
            ir = jnp.where(row == 0, pr, 0.0)
            ii = jnp.where(row == 0, pi, 0.0)
            for seg in range(1, S5_SEGS):
                er = fr[k] + sr * ir - si * ii
                ei = fi[k] + sr * ii + si * ir
                ir = jnp.where(row == seg, pltpu.roll(er, 1, axis=0), ir)
                ii = jnp.where(row == seg, pltpu.roll(ei, 1, axis=0), ii)
            er = fr[k] + sr * ir - si * ii
            ei = fi[k] + sr * ii + si * ir
            init_r.append(ir)
            init_i.append(ii)
            new_r.append(pltpu.roll(er, 1, axis=0))
            new_i.append(pltpu.roll(ei, 1, axis=0))
        for k in range(half):
            st_ref[j, :, k * LANES:(k + 1) * LANES] = new_r[k]
            st_ref[j, :, (half + k) * LANES:(half + k + 1) * LANES] = new_i[k]

        def step2(s2, x):
            xr, xn = x
            outs_r, outs_n = [], []
            for _ in range(2):
                outs_r.append([])
                outs_n.append([])
            for t in range(2):
                s = s2 * 2 + t
                nr, nn = [], []
                for k in range(half):
                    br = load(s, k)
                    bi = load(s, half + k)
                    nr.append(a_r[k] * xr[k] + a_i[k] * xn[k] + br)
                    nn.append(a_r[k] * xn[k] - a_i[k] * xr[k] - bi)
                xr, xn = tuple(nr), tuple(nn)
                outs_r[t] = nr
                outs_n[t] = nn
            r0 = pl.multiple_of(s2 * (2 * S5_SEGS), 2 * S5_SEGS)
            for k in range(half):
                xp_ref[j, pl.ds(r0, 2 * S5_SEGS), k * LANES:(k + 1) * LANES] = (
                    jnp.concatenate([outs_r[0][k], outs_r[1][k]], axis=0).astype(BF16))
                xp_ref[j, pl.ds(r0, 2 * S5_SEGS), (half + k) * LANES:(half + k + 1) * LANES] = (
                    jnp.concatenate([outs_n[0][k], outs_n[1][k]], axis=0).astype(BF16))
            return xr, xn

        neg_i = tuple(-v for v in init_i)
        lax.fori_loop(0, S5_SEG // 2, step2, (tuple(init_r), neg_i), unroll=2)
        return carry

    lax.fori_loop(0, n_tiles, tile_scan, 0)

    for jj in range(n_tiles // 2):
        yy = (jnp.dot(xp_ref[2 * jj], ca_ref[jj], preferred_element_type=F32)
              + jnp.dot(xp_ref[2 * jj + 1], cb_ref[jj], preferred_element_type=F32))
        yp_ref[2 * jj] = yy[:, :LANES]
        yp_ref[2 * jj + 1] = yy[:, LANES:]

    for j in range(n_tiles):
        for seg in range(S5_SEGS):
            yn_ref[seg * S5_SEG:(seg + 1) * S5_SEG, j * LANES:(j + 1) * LANES] = (
                yp_ref[j, pl.ds(seg, S5_SEG, stride=S5_SEGS), :])

    y = yn_ref[...] + d_ref[...] * u
    y = jax.nn.gelu(y)
    z = jnp.dot(y.astype(BF16), wglu_ref[...], preferred_element_type=F32) + bglu_ref[...]
    o_ref[...] = (y * jax.nn.sigmoid(z)).astype(o_ref.dtype)


def _s5_layouts(ar, ai, asr, asi, bbr, bbi, c_re, c_im):
    g, p = ar.shape
    tg = S5_TILE_GROUPS
    nt = g // tg
    eye = jnp.eye(tg, dtype=F32)

    def lanes(a):
        return jnp.broadcast_to(a.reshape(nt, 1, tg * p), (nt, SUBLANES, tg * p))

    bb = jnp.stack([bbr, bbi]).reshape(2, S5_GROUP_SIZE, nt, tg, p)
    bb = jnp.transpose(bb, (2, 3, 1, 0, 4))
    bblk = bb[:, :, :, :, None, :] * eye[None, :, None, None, :, None]
    bblk = bblk.reshape(nt, tg * S5_GROUP_SIZE, 2 * tg * p).astype(BF16)

    cc = jnp.stack([c_re, c_im]).reshape(2, nt, tg, S5_GROUP_SIZE, p)
    cc = jnp.transpose(cc, (1, 0, 2, 4, 3))
    cblk = cc[:, :, :, :, None, :] * eye[None, None, :, None, :, None]
    cblk = cblk.reshape(nt, 2 * tg * p, tg * S5_GROUP_SIZE)
    zeros = jnp.zeros_like(cblk[0::2])
    ca = jnp.concatenate([cblk[0::2], zeros], axis=-1).astype(BF16)
    cb = jnp.concatenate([zeros, cblk[1::2]], axis=-1).astype(BF16)
    return bblk, ca, cb, lanes(ar), lanes(ai), lanes(asr), lanes(asi)


def s5_mixer(proj, batch, params, d_skip, w_glu, b_glu):
    bblk, ca, cb, ar, ai, asr, asi = params
    m = proj.shape[0]
    width = w_glu.shape[0]
    nt = width // LANES
    steps = m // batch // S5_ROWS
    slabs = 2 * S5_TILE_STATES // LANES
    const3 = lambda b, n: (0, 0, 0)
    const2 = lambda b, n: (0, 0)
    return pl.pallas_call(
        _s5_kernel,
        grid=(batch, steps),
        in_specs=[
            pl.BlockSpec((S5_ROWS, width), lambda b, n: (b * steps + n, 0)),
            pl.BlockSpec(bblk.shape, const3),
            pl.BlockSpec(ca.shape, const3),
            pl.BlockSpec(cb.shape, const3),
            pl.BlockSpec(ar.shape, const3),
            pl.BlockSpec(ai.shape, const3),
            pl.BlockSpec(asr.shape, const3),
            pl.BlockSpec(asi.shape, const3),
            pl.BlockSpec((1, width), const2),
            pl.BlockSpec((width, width), const2),
            pl.BlockSpec((1, width), const2),
        ],
        out_specs=pl.BlockSpec((S5_ROWS, width), lambda b, n: (b * steps + n, 0)),
        out_shape=jax.ShapeDtypeStruct((m, width), BF16),
        scratch_shapes=[
            pltpu.VMEM((nt * slabs, S5_SEGS * S5_PITCH, LANES), F32),
            pltpu.VMEM((nt, S5_ROWS, 2 * S5_TILE_STATES), BF16),
            pltpu.VMEM((nt, S5_ROWS, LANES), F32),
            pltpu.VMEM((S5_ROWS, width), F32),
            pltpu.VMEM((nt, SUBLANES, 2 * S5_TILE_STATES), F32),
        ],
        compiler_params=_cparams("arbitrary", "arbitrary"),
        name="s5_mixer",
    )(proj, bblk, ca, cb, ar, ai, asr, asi, d_skip.reshape(1, width),
      w_glu.astype(BF16), b_glu.reshape(1, width))


def _hgrn_kernel(q_ref, f_ref, i_ref, g_ref, lbp_ref, og_ref, cum_ref, o_ref, st_ref, *, layer):
    n_levels = int(math.log2(HGRN_CHUNK))
    c = HGRN_CHUNK

    @pl.when(pl.program_id(2) == 0)
    def _():
        st_ref[...] = jnp.zeros_like(st_ref)

    lbp = lbp_ref[...]
    e = jnp.exp(lbp - jnp.max(lbp, axis=0, keepdims=True))
    prob = e / jnp.sum(e, axis=0, keepdims=True)
    lb = jnp.sum(prob[:layer + 1], axis=0, keepdims=True) - prob[0:1]
    og = og_ref[...]

    ti = lax.broadcasted_iota(jnp.int32, (c, c), 0)
    si = lax.broadcasted_iota(jnp.int32, (c, c), 1)
    diff_bits = ti ^ si
    cum = cum_ref[...]

    def chunk(ci, carry):
        r = pl.multiple_of(ci * c, c)
        q = q_ref[pl.ds(r, c), :]
        f = f_ref[pl.ds(r, c), :]
        v = i_ref[pl.ds(r, c), :]
        g = g_ref[pl.ds(r, c), :]
        forget = lb + (1.0 - lb) * jax.nn.sigmoid(f)
        logf = jnp.log(forget)
        kk = 1.0 - forget
        qq = jax.nn.silu(q)

        hi, lo = _split_bf16(logf)
        dec = (jnp.dot(cum, hi, preferred_element_type=F32)
               + jnp.dot(cum, lo, preferred_element_type=F32))

        scores = jnp.zeros((c, c), F32)
        for l in range(n_levels):
            eq = logf if l == 0 else dec[(l - 1) * c:l * c]
            ql = (qq * jnp.exp(eq)).astype(BF16)
            if l == 0:
                kl = kk.astype(BF16)
            else:
                kl = (kk * jnp.exp(dec[(n_levels + l - 1) * c:(n_levels + l) * c])).astype(BF16)
            sl = lax.dot_general(ql, kl, (((1,), (1,)), ((), ())), preferred_element_type=F32)
            mask = ((diff_bits >> l) == 1) & (ti > si)
            scores = jnp.where(mask, sl, scores)
        dsum = jnp.sum(qq * kk, axis=-1, keepdims=True)
        scores = jnp.where(ti == si, dsum, scores)

        b = dec[(n_levels - 1) * c:n_levels * c]
        to_end = dec[(2 * n_levels - 1) * c:2 * n_levels * c]
        b_last = b[c - 1:c, :]
        st = st_ref[...]
        out = (jnp.dot(scores.astype(BF16), v.astype(BF16), preferred_element_type=F32)
               + lax.dot_general((qq * jnp.exp(b)).astype(BF16), st.astype(BF16),
                                 (((1,), (1,)), ((), ())), preferred_element_type=F32))
        kd = (kk * jnp.exp(to_end)).astype(BF16)
        st_ref[...] = (st * jnp.exp(b_last)
                       + lax.dot_general(v.astype(BF16), kd, (((0,), (0,)), ((), ())),
                                         preferred_element_type=F32))

        ms = jnp.mean(out * out, axis=-1, keepdims=True)
        o_ref[pl.ds(r, c), :] = (out * lax.rsqrt(ms + EPS) * og * jax.nn.silu(g)).astype(o_ref.dtype)
        return carry

    lax.fori_loop(0, q_ref.shape[0] // c, chunk, 0)


def _hgrn_cum_matrices():
    c = HGRN_CHUNK
    n_levels = int(math.log2(c))
    t = jnp.arange(c)[:, None]
    r = jnp.arange(c)[None, :]
    mats = []
    for l in range(1, n_levels + 1):
        mats.append(((t >> l) == (r >> l)) & (r <= t))
    for l in range(1, n_levels + 1):
        mats.append(((t >> l) == (r >> l)) & (r > t))
    return jnp.concatenate(mats, axis=0).astype(BF16)


def hgrn_mixer(proj, batch, col0, lb_param, o_gain, layer):
    m = proj.shape[0]
    n_layers, width = lb_param.shape
    heads = width // HGRN_HEAD_DIM
    steps = m // batch // HGRN_ROWS
    hd = HGRN_HEAD_DIM
    cum = _hgrn_cum_matrices()

    def col(block):
        return pl.BlockSpec((HGRN_ROWS, hd),
                            lambda b, h, n: (b * steps + n, (col0 + block * width) // hd + h))

    return pl.pallas_call(
        functools.partial(_hgrn_kernel, layer=layer),
        grid=(batch, heads, steps),
        in_specs=[
            col(0), col(1), col(2), col(3),
            pl.BlockSpec((n_layers, hd), lambda b, h, n: (0, h)),
            pl.BlockSpec((1, hd), lambda b, h, n: (0, 0)),
            pl.BlockSpec(cum.shape, lambda b, h, n: (0, 0)),
        ],
        out_specs=pl.BlockSpec((HGRN_ROWS, hd), lambda b, h, n: (b * steps + n, h)),
        out_shape=jax.ShapeDtypeStruct((m, width), BF16),
        scratch_shapes=[pltpu.VMEM((hd, hd), F32)],
        compiler_params=_cparams("arbitrary", "arbitrary", "arbitrary"),
        name="hgrn_mixer",
    )(proj, proj, proj, proj, lb_param, o_gain.reshape(1, hd), cum)


def _attn_kernel(sink_ref, slope_ref, q_ref, kc_ref, vc_ref, kp_ref, vp_ref,
                 qg_ref, kg_ref, ind_ref, indt_ref, o_ref, *, n_kv):
    blk = ATT_BLOCK
    hd = ATT_HEAD_DIM
    nblk = pl.program_id(1)
    ind = ind_ref[...]
    indt = indt_ref[...]

    def head_rms_scale(x):
        w = x.shape[1]
        hi, lo = _split_bf16(x * x)
        ss = (jnp.dot(hi, ind[:w], preferred_element_type=F32)
              + jnp.dot(lo, ind[:w], preferred_element_type=F32))
        rs = lax.rsqrt(ss * (1.0 / hd) + EPS)
        hi, lo = _split_bf16(rs)
        return (jnp.dot(hi, indt[:, :w], preferred_element_type=F32)
                + jnp.dot(lo, indt[:, :w], preferred_element_type=F32))

    q = q_ref[...]
    qn = (q * head_rms_scale(q) * qg_ref[...] * (hd ** -0.5)).astype(BF16)
    k = jnp.concatenate([kp_ref[...], kc_ref[...]], axis=0)
    kn = k * head_rms_scale(k) * kg_ref[...]
    v = jnp.concatenate([vp_ref[...], vc_ref[...]], axis=0)

    t = lax.broadcasted_iota(jnp.int32, (blk, 2 * blk), 0)
    s = lax.broadcasted_iota(jnp.int32, (blk, 2 * blk), 1)
    dist = t + blk - s
    valid = (dist >= 0) & (dist < WINDOW) & ((s >= blk) | (nblk > 0))
    negdist = -dist.astype(F32)
    lane = lax.broadcasted_iota(jnp.int32, (2 * blk, LANES), 1)

    for kv in range(n_kv):
        tile = kv // 2
        k_t = kn[:, tile * LANES:(tile + 1) * LANES]
        v_t = v[:, tile * LANES:(tile + 1) * LANES]
        own = (lane < hd) if kv % 2 == 0 else (lane >= hd)
        k_own = jnp.where(own, k_t, 0.0)
        v_own = jnp.where(own, v_t, 0.0)
        k_half = [None, None]
        v_half = [None, None]
        k_half[kv % 2] = k_own.astype(BF16)
        v_half[kv % 2] = v_own.astype(BF16)
        k_half[1 - kv % 2] = pltpu.roll(k_own, hd, axis=1).astype(BF16)
        v_half[1 - kv % 2] = pltpu.roll(v_own, hd, axis=1).astype(BF16)
        for mt in range(ATT_GROUP // 2):
            qt = kv * (ATT_GROUP // 2) + mt
            q_t = qn[:, qt * LANES:(qt + 1) * LANES]
            acc = jnp.zeros((blk, LANES), F32)
            for par in range(2):
                h = 2 * qt + par
                sc = lax.dot_general(q_t, k_half[par], (((1,), (1,)), ((), ())),
                                     preferred_element_type=F32)
                sc = jnp.where(valid, sc + slope_ref[h] * negdist, -jnp.inf)
                sink = sink_ref[h]
                mx = jnp.maximum(jnp.max(sc, axis=-1, keepdims=True), sink)
                p = jnp.exp(sc - mx)
                den = jnp.sum(p, axis=-1, keepdims=True) + jnp.exp(sink - mx)
                acc = acc + jnp.dot((p / den).astype(BF16), v_half[par],
                                    preferred_element_type=F32)
            o_ref[:, qt * LANES:(qt + 1) * LANES] = acc.astype(o_ref.dtype)


def attention(qkv, batch, n_heads, n_kv, q_gain, k_gain, sinks):
    m = qkv.shape[0]
    hd = ATT_HEAD_DIM
    qw = n_heads * hd
    kw = n_kv * hd
    steps = m // batch // ATT_BLOCK
    slopes = jnp.exp2(-8.0 * jnp.arange(1, n_heads + 1, dtype=F32) / n_heads)
    head_of_lane = jnp.arange(qw) // hd
    ind = (head_of_lane[:, None] == jnp.arange(LANES)[None, :]).astype(BF16)
    cur = lambda b, n, *_: b * steps + n
    prev = lambda b, n, *_: b * steps + jnp.maximum(n - 1, 0)
    grid_spec = pltpu.PrefetchScalarGridSpec(
        num_scalar_prefetch=2,
        grid=(batch, steps),
        in_specs=[
            pl.BlockSpec((ATT_BLOCK, qw), lambda b, n, *_: (cur(b, n), 0)),
            pl.BlockSpec((ATT_BLOCK, kw), lambda b, n, *_: (cur(b, n), qw // kw)),
            pl.BlockSpec((ATT_BLOCK, kw), lambda b, n, *_: (cur(b, n), qw // kw + 1)),
            pl.BlockSpec((ATT_BLOCK, kw), lambda b, n, *_: (prev(b, n), qw // kw)),
            pl.BlockSpec((ATT_BLOCK, kw), lambda b, n, *_: (prev(b, n), qw // kw + 1)),
            pl.BlockSpec((1, qw), lambda b, n, *_: (0, 0)),
            pl.BlockSpec((1, kw), lambda b, n, *_: (0, 0)),
            pl.BlockSpec(ind.shape, lambda b, n, *_: (0, 0)),
            pl.BlockSpec(ind.shape[::-1], lambda b, n, *_: (0, 0)),
        ],
        out_specs=pl.BlockSpec((ATT_BLOCK, qw), lambda b, n, *_: (cur(b, n), 0)),
    )
    return pl.pallas_call(
        functools.partial(_attn_kernel, n_kv=n_kv),
        grid_spec=grid_spec,
        out_shape=jax.ShapeDtypeStruct((m, qw), BF16),
        compiler_params=_cparams("arbitrary", "arbitrary"),
        name="swa_attention",
    )(sinks, slopes, qkv, qkv, qkv, qkv, qkv,
      jnp.tile(q_gain, n_heads).reshape(1, qw), jnp.tile(k_gain, n_kv).reshape(1, kw),
      ind, ind.T)


def kernel(x, even_norm, even_w_in, s5_lambda_re, s5_lambda_im, s5_log_dt, s5_b_re, s5_b_im,
           s5_c_re, s5_c_im, s5_d, s5_w_glu, s5_b_glu, hgrn_lower_bound, hgrn_o_norm, even_w_out,
           odd_norm, odd_w_qkv, q_norm, k_norm, att_sinks, odd_w_out, mlp_norm, mlp_w_up, mlp_w_down):
    batch, seqlen, d_model = x.shape
    depth = mlp_norm.shape[0]
    s5_width = s5_w_glu.shape[1]
    n_heads = att_sinks.shape[1]
    n_kv = n_heads // ATT_GROUP
    assert seqlen % max(S5_ROWS, HGRN_ROWS, ATT_BLOCK) == 0
    h = x.reshape(batch * seqlen, d_model)
    for layer in range(depth):
        j = layer // 2
        if layer % 2 == 0:
            proj = norm_matmul(h, even_norm[j], even_w_in[j], tn=512)
            prep = s5_prep(s5_lambda_re[j], s5_lambda_im[j], s5_log_dt[j], s5_b_re[j], s5_b_im[j])
            params = _s5_layouts(*prep, s5_c_re[j], s5_c_im[j])
            y_a = s5_mixer(proj, batch, params, s5_d[j], s5_w_glu[j], s5_b_glu[j])
            y_b = hgrn_mixer(proj, batch, s5_width, hgrn_lower_bound, hgrn_o_norm[j], j)
            h = proj_residual([y_a, y_b], even_w_out[j], h, tn=512)
        else:
            qkv = norm_matmul(h, odd_norm[j], odd_w_qkv[j], tn=512)
            o = attention(qkv, batch, n_heads, n_kv, q_norm[j], k_norm[j], att_sinks[j])
            h = proj_residual([o], odd_w_out[j], h, tn=512)
        h = mlp_residual(h, mlp_norm[layer], mlp_w_up[layer], mlp_w_down[layer], tf=256)
    return h.reshape(batch, seqlen, d_model)
```

```python
import functools
import math

import jax
import jax.numpy as jnp
from jax import lax
from jax.experimental import pallas as pl
from jax.experimental.pallas import tpu as pltpu

F32 = jnp.float32
BF16 = jnp.bfloat16
EPS = 1e-6
LOG2E = math.log2(math.e)

LANES = 128
SUBLANES = 8
VMEM_LIMIT = 56 * 1024 * 1024

S5_GROUP_SIZE = 16
S5_STATE = 64
S5_MIN_DECAY = 1e-4
HGRN_HEAD_DIM = 128
HGRN_CHUNK = 64
ATT_HEAD_DIM = 64
ATT_GROUP = 8
WINDOW = 128
ATT_BLOCK = 128

ROW_TILE = 1024
NORM_ROWS = 128
S5_ROWS = 256
S5_SEGS = SUBLANES
S5_SEG = S5_ROWS // S5_SEGS
S5_PITCH = S5_SEG + 8
S5_TILE_GROUPS = LANES // S5_GROUP_SIZE
S5_TILE_STATES = S5_TILE_GROUPS * S5_STATE
HGRN_ROWS = 512


def _cparams(*sem):
    return pltpu.CompilerParams(dimension_semantics=sem, vmem_limit_bytes=VMEM_LIMIT)


def _split_bf16(x):
    hi = x.astype(BF16)
    lo = (x - hi.astype(F32)).astype(BF16)
    return hi, lo


def _rmsnorm_rows(h_ref, g_ref, xn_ref):
    rows = h_ref.shape[0]
    g = g_ref[...]

    def body(i, carry):
        r = pl.multiple_of(i * NORM_ROWS, NORM_ROWS)
        x = h_ref[pl.ds(r, NORM_ROWS), :]
        ms = jnp.mean(x * x, axis=-1, keepdims=True)
        xn_ref[pl.ds(r, NORM_ROWS), :] = (x * lax.rsqrt(ms + EPS) * g).astype(BF16)
        return carry

    lax.fori_loop(0, rows // NORM_ROWS, body, 0)


def _norm_matmul_kernel(h_ref, g_ref, w_ref, o_ref, xn_ref):
    @pl.when(pl.program_id(1) == 0)
    def _():
        _rmsnorm_rows(h_ref, g_ref, xn_ref)

    o_ref[...] = jnp.dot(xn_ref[...], w_ref[...].astype(BF16), preferred_element_type=F32)


def norm_matmul(h, gain, w_stack, layer, tn):
    m, k = h.shape
    n = w_stack.shape[2]
    tm = min(ROW_TILE, m)
    return pl.pallas_call(
        _norm_matmul_kernel,
        grid=(m // tm, n // tn),
        in_specs=[
            pl.BlockSpec((tm, k), lambda i, j: (i, 0)),
            pl.BlockSpec((1, k), lambda i, j: (0, 0)),
            pl.BlockSpec((None, k, tn), lambda i, j: (layer, 0, j)),
        ],
        out_specs=pl.BlockSpec((tm, tn), lambda i, j: (i, j)),
        out_shape=jax.ShapeDtypeStruct((m, n), F32),
        scratch_shapes=[pltpu.VMEM((tm, k), BF16)],
        compiler_params=_cparams("parallel", "arbitrary"),
        name="norm_matmul",
    )(h, gain.reshape(1, k), w_stack)


def _proj_residual_kernel(*refs, n_lhs):
    lhs_refs = refs[:n_lhs]
    w_ref, res_ref, o_ref = refs[n_lhs:]
    acc = res_ref[...]
    k0 = 0
    for lhs_ref in lhs_refs:
        kk = lhs_ref.shape[1]
        acc = acc + jnp.dot(lhs_ref[...], w_ref[k0:k0 + kk, :].astype(BF16),
                            preferred_element_type=F32)
        k0 += kk
    o_ref[...] = acc


def proj_residual(lhs_list, w_stack, layer, res, tn):
    m, n = res.shape
    k = w_stack.shape[1]
    tm = min(ROW_TILE, m)
    in_specs = [pl.BlockSpec((tm, x.shape[1]), lambda i, j: (i, 0)) for x in lhs_list]
    in_specs += [pl.BlockSpec((None, k, tn), lambda i, j: (layer, 0, j)),
                 pl.BlockSpec((tm, tn), lambda i, j: (i, j))]
    return pl.pallas_call(
        functools.partial(_proj_residual_kernel, n_lhs=len(lhs_list)),
        grid=(m // tm, n // tn),
        in_specs=in_specs,
        out_specs=pl.BlockSpec((tm, tn), lambda i, j: (i, j)),
        out_shape=jax.ShapeDtypeStruct((m, n), F32),
        compiler_params=_cparams("parallel", "arbitrary"),
        name="proj_residual",
    )(*lhs_list, w_stack, res)


def _mlp_kernel(h_ref, g_ref, wu_ref, wd_ref, o_ref, xn_ref):
    @pl.when(pl.program_id(1) == 0)
    def _():
        _rmsnorm_rows(h_ref, g_ref, xn_ref)
        o_ref[...] = h_ref[...]

    up = jnp.dot(xn_ref[...], wu_ref[...].astype(BF16), preferred_element_type=F32)
    act = jnp.square(jnp.maximum(up, 0.0)).astype(BF16)
    o_ref[...] += jnp.dot(act, wd_ref[...].astype(BF16), preferred_element_type=F32)


def mlp_residual(h, gain, w_up_stack, w_down_stack, layer, tf):
    m, d = h.shape
    f = w_up_stack.shape[2]
    tm = min(ROW_TILE, m)
    once = pl.Buffered(1)
    return pl.pallas_call(
        _mlp_kernel,
        grid=(m // tm, f // tf),
        in_specs=[
            pl.BlockSpec((tm, d), lambda i, j: (i, 0), pipeline_mode=once),
            pl.BlockSpec((1, d), lambda i, j: (0, 0)),
            pl.BlockSpec((None, d, tf), lambda i, j: (layer, 0, j)),
            pl.BlockSpec((None, tf, d), lambda i, j: (layer, j, 0)),
        ],
        out_specs=pl.BlockSpec((tm, d), lambda i, j: (i, 0), pipeline_mode=once),
        out_shape=jax.ShapeDtypeStruct((m, d), F32),
        scratch_shapes=[pltpu.VMEM((tm, d), BF16)],
        compiler_params=_cparams("parallel", "arbitrary"),
        name="mlp_residual",
    )(h, gain.reshape(1, d), w_up_stack, w_down_stack)


def _s5_prep_kernel(lre_ref, lim_ref, ldt_ref, bre_ref, bim_ref,
                    ar_ref, ai_ref, asr_ref, asi_ref, bbr_ref, bbi_ref):
    lr = jnp.minimum(lre_ref[...], -S5_MIN_DECAY)
    li = lim_ref[...]
    dt = jnp.exp(ldt_ref[...])
    mag = jnp.exp(lr * dt)
    ar = mag * jnp.cos(li * dt)
    ai = mag * jnp.sin(li * dt)
    den = lr * lr + li * li
    zr = ((ar - 1.0) * lr + ai * li) / den
    zi = (ai * lr - (ar - 1.0) * li) / den
    ar_ref[...] = ar
    ai_ref[...] = ai
    pr, pi = ar, ai
    for _ in range(int(math.log2(S5_SEG))):
        pr, pi = pr * pr - pi * pi, 2.0 * pr * pi
    asr_ref[...] = pr
    asi_ref[...] = pi
    for c in range(S5_GROUP_SIZE):
        br = bre_ref[c]
        bi = bim_ref[c]
        bbr_ref[c] = zr * br - zi * bi
        bbi_ref[c] = zr * bi + zi * br


def s5_prep(lam_re, lam_im, log_dt, b_re, b_im):
    g, p = lam_re.shape
    ldt = jnp.broadcast_to(log_dt[:, None], (g, p))
    bre_t = jnp.transpose(b_re, (2, 0, 1))
    bim_t = jnp.transpose(b_im, (2, 0, 1))
    gp = jax.ShapeDtypeStruct((g, p), F32)
    cgp = jax.ShapeDtypeStruct((S5_GROUP_SIZE, g, p), F32)
    return pl.pallas_call(
        _s5_prep_kernel,
        out_shape=(gp, gp, gp, gp, cgp, cgp),
        name="s5_prep",
    )(lam_re, lam_im, ldt, bre_t, bim_t)


def _s5_kernel(u_ref, bblk_ref, ca_ref, cb_ref, ar_ref, ai_ref, asr_ref, asi_ref,
               d_ref, wglu_ref, bglu_ref, o_ref,
               bu_ref, xp_ref, yp_ref, yn_ref, st_ref):
    n_tiles = u_ref.shape[1] // LANES
    half = S5_TILE_STATES // LANES
    slabs = 2 * half

    @pl.when(pl.program_id(1) == 0)
    def _():
        st_ref[...] = jnp.zeros_like(st_ref)

    u = u_ref[...]
    ub = u.astype(BF16)
    for j in range(n_tiles):
        res = jnp.dot(ub[:, j * LANES:(j + 1) * LANES], bblk_ref[j], preferred_element_type=F32)
        for seg in range(S5_SEGS):
            for k in range(slabs):
                bu_ref[j * slabs + k, seg * S5_PITCH:seg * S5_PITCH + S5_SEG, :] = (
                    res[seg * S5_SEG:(seg + 1) * S5_SEG, k * LANES:(k + 1) * LANES])

    row = lax.broadcasted_iota(jnp.int32, (SUBLANES, LANES), 0)

    def tile_scan(j, carry):
        a_r = [ar_ref[j, :, k * LANES:(k + 1) * LANES] for k in range(half)]
        a_i = [ai_ref[j, :, k * LANES:(k + 1) * LANES] for k in range(half)]

        def load(s, k):
            return bu_ref[j * slabs + k, pl.ds(s, S5_SEGS, stride=S5_PITCH), :]

        def step1(s, x):
            xr, xi = x
            nr, ni = [], []
            for k in range(half):
                br = load(s, k)
                bi = load(s, half + k)
                nr.append(a_r[k] * xr[k] - a_i[k] * xi[k] + br)
                ni.append(a_r[k] * xi[k] + a_i[k] * xr[k] + bi)
            return tuple(nr), tuple(ni)

        zero = tuple(jnp.zeros((SUBLANES, LANES), F32) for _ in range(half))
        fr, fi = lax.fori_loop(0, S5_SEG, step1, (zero, zero), unroll=4)

        init_r, init_i, new_r, new_i = [], [], [], []
        for k in range(half):
            sr = asr_ref[j, :, k * LANES:(k + 1) * LANES]
            si = asi_ref[j, :, k * LANES:(k + 1) * LANES]
            pr = st_ref[j, :, k * LANES:(k + 1) * LANES]
            pi = st_ref[j, :, (half + k) * LANES:(half + k + 1) * LANES]
            ir = jnp.where(row == 0, pr, 0.0)
            ii = jnp.where(row == 0, pi, 0.0)
            for seg in range(1, S5_SEGS):
                er = fr[k] + sr * ir - si * ii
                ei = fi[k] + sr * ii + si * ir
                ir = jnp.where(row == seg, pltpu.roll(er, 1, axis=0), ir)
                ii = jnp.where(row == seg, pltpu.roll(ei, 1, axis=0), ii)
            er = fr[k] + sr * ir - si * ii
            ei = fi[k] + sr * ii + si * ir
            init_r.append(ir)
            init_i.append(ii)
            new_r.append(pltpu.roll(er, 1, axis=0))
            new_i.append(pltpu.roll(ei, 1, axis=0))
        for k in range(half):
            st_ref[j, :, k * LANES:(k + 1) * LANES] = new_r[k]
            st_ref[j, :, (half + k) * LANES:(half + k + 1) * LANES] = new_i[k]

        def step2(s2, x):
            xr, xn = x
            outs_r, outs_n = [], []
            for _ in range(2):
                outs_r.append([])
                outs_n.append([])
            for t in range(2):
                s = s2 * 2 + t
                nr, nn = [], []
                for k in range(half):
                    br = load(s, k)
                    bi = load(s, half + k)
                    nr.append(a_r[k] * xr[k] + a_i[k] * xn[k] + br)
                    nn.append(a_r[k] * xn[k] - a_i[k] * xr[k] - bi)
                xr, xn = tuple(nr), tuple(nn)
                outs_r[t] = nr
                outs_n[t] = nn
            r0 = pl.multiple_of(s2 * (2 * S5_SEGS), 2 * S5_SEGS)
            for k in range(half):
                xp_ref[j, pl.ds(r0, 2 * S5_SEGS), k * LANES:(k + 1) * LANES] = (
                    jnp.concatenate([outs_r[0][k], outs_r[1][k]], axis=0).astype(BF16))
                xp_ref[j, pl.ds(r0, 2 * S5_SEGS), (half + k) * LANES:(half + k + 1) * LANES] = (
                    jnp.concatenate([outs_n[0][k], outs_n[1][k]], axis=0).astype(BF16))
            return xr, xn

        neg_i = tuple(-v for v in init_i)
        lax.fori_loop(0, S5_SEG // 2, step2, (tuple(init_r), neg_i), unroll=2)
        return carry

    lax.fori_loop(0, n_tiles, tile_scan, 0)

    for jj in range(n_tiles // 2):
        yy = (jnp.dot(xp_ref[2 * jj], ca_ref[jj], preferred_element_type=F32)
              + jnp.dot(xp_ref[2 * jj + 1], cb_ref[jj], preferred_element_type=F32))
        yp_ref[2 * jj] = yy[:, :LANES]
        yp_ref[2 * jj + 1] = yy[:, LANES:]

    for j in range(n_tiles):
        for seg in range(S5_SEGS):
            yn_ref[seg * S5_SEG:(seg + 1) * S5_SEG, j * LANES:(j + 1) * LANES] = (
                yp_ref[j, pl.ds(seg, S5_SEG, stride=S5_SEGS), :])

    y = yn_ref[...] + d_ref[...] * u
    y = jax.nn.gelu(y)
    z = jnp.dot(y.astype(BF16), wglu_ref[...], preferred_element_type=F32) + bglu_ref[...]
    o_ref[...] = (y * jax.nn.sigmoid(z)).astype(o_ref.dtype)


def _s5_layouts(ar, ai, asr, asi, bbr, bbi, c_re, c_im):
    g, p = ar.shape
    tg = S5_TILE_GROUPS
    nt = g // tg
    eye = jnp.eye(tg, dtype=F32)

    def lanes(a):
        return jnp.broadcast_to(a.reshape(nt, 1, tg * p), (nt, SUBLANES, tg * p))

    bb = jnp.stack([bbr, bbi]).reshape(2, S5_GROUP_SIZE, nt, tg, p)
    bb = jnp.transpose(bb, (2, 3, 1, 0, 4))
    bblk = bb[:, :, :, :, None, :] * eye[None, :, None, None, :, None]
    bblk = bblk.reshape(nt, tg * S5_GROUP_SIZE, 2 * tg * p).astype(BF16)

    cc = jnp.stack([c_re, c_im]).reshape(2, nt, tg, S5_GROUP_SIZE, p)
    cc = jnp.transpose(cc, (1, 0, 2, 4, 3))
    cblk = cc[:, :, :, :, None, :] * eye[None, None, :, None, :, None]
    cblk = cblk.reshape(nt, 2 * tg * p, tg * S5_GROUP_SIZE)
    zeros = jnp.zeros_like(cblk[0::2])
    ca = jnp.concatenate([cblk[0::2], zeros], axis=-1).astype(BF16)
    cb = jnp.concatenate([zeros, cblk[1::2]], axis=-1).astype(BF16)
    return bblk, ca, cb, lanes(ar), lanes(ai), lanes(asr), lanes(asi)


def s5_mixer(proj, batch, params, d_skip, w_glu, b_glu):
    bblk, ca, cb, ar, ai, asr, asi = params
    m = proj.shape[0]
    width = w_glu.shape[0]
    nt = width // LANES
    steps = m // batch // S5_ROWS
    slabs = 2 * S5_TILE_STATES // LANES
    const3 = lambda b, n: (0, 0, 0)
    const2 = lambda b, n: (0, 0)
    return pl.pallas_call(
        _s5_kernel,
        grid=(batch, steps),
        in_specs=[
            pl.BlockSpec((S5_ROWS, width), lambda b, n: (b * steps + n, 0)),
            pl.BlockSpec(bblk.shape, const3),
            pl.BlockSpec(ca.shape, const3),
            pl.BlockSpec(cb.shape, const3),
            pl.BlockSpec(ar.shape, const3),
            pl.BlockSpec(ai.shape, const3),
            pl.BlockSpec(asr.shape, const3),
            pl.BlockSpec(asi.shape, const3),
            pl.BlockSpec((1, width), const2),
            pl.BlockSpec((width, width), const2),
            pl.BlockSpec((1, width), const2),
        ],
        out_specs=pl.BlockSpec((S5_ROWS, width), lambda b, n: (b * steps + n, 0)),
        out_shape=jax.ShapeDtypeStruct((m, width), BF16),
        scratch_shapes=[
            pltpu.VMEM((nt * slabs, S5_SEGS * S5_PITCH, LANES), F32),
            pltpu.VMEM((nt, S5_ROWS, 2 * S5_TILE_STATES), BF16),
            pltpu.VMEM((nt, S5_ROWS, LANES), F32),
            pltpu.VMEM((S5_ROWS, width), F32),
            pltpu.VMEM((nt, SUBLANES, 2 * S5_TILE_STATES), F32),
        ],
        compiler_params=_cparams("arbitrary", "arbitrary"),
        name="s5_mixer",
    )(proj, bblk, ca, cb, ar, ai, asr, asi, d_skip.reshape(1, width),
      w_glu.astype(BF16), b_glu.reshape(1, width))


def _hgrn_kernel(q_ref, f_ref, i_ref, g_ref, lbp_ref, og_ref, cum_ref, o_ref, st_ref, *, layer):
    n_levels = int(math.log2(HGRN_CHUNK))
    c = HGRN_CHUNK
    hd = HGRN_HEAD_DIM

    @pl.when(pl.program_id(2) == 0)
    def _():
        st_ref[...] = jnp.zeros_like(st_ref)

    lbp = lbp_ref[...]
    e = jnp.exp(lbp - jnp.max(lbp, axis=0, keepdims=True))
    prob = e / jnp.sum(e, axis=0, keepdims=True)
    lb = jnp.sum(prob[:layer + 1], axis=0, keepdims=True) - prob[0:1]
    og = og_ref[...]

    rows = q_ref.shape[0]
    n_chunks = rows // c
    ti = lax.broadcasted_iota(jnp.int32, (c, c), 0)
    si = lax.broadcasted_iota(jnp.int32, (c, c), 1)
    diff_bits = ti ^ si
    tril = cum_ref[...]
    tmod = lax.broadcasted_iota(jnp.int32, (rows, hd), 0) & (c - 1)
    nt_dims = (((1,), (1,)), ((), ()))

    def chunk_rows(x, ch):
        return x[ch * c:(ch + 1) * c]

    forget = lb + (1.0 - lb) * jax.nn.sigmoid(f_ref[...])
    logf = jnp.log(forget)
    kk = 1.0 - forget
    qq = jax.nn.silu(q_ref[...])
    vb = i_ref[...].astype(BF16)

    hi, lo = _split_bf16(logf)
    b = jnp.concatenate(
        [jnp.dot(tril, chunk_rows(hi, ch), preferred_element_type=F32)
         + jnp.dot(tril, chunk_rows(lo, ch), preferred_element_type=F32) for ch in range(n_chunks)],
        axis=0)

    block_end = b
    scores = [jnp.zeros((c, c), F32) for _ in range(n_chunks)]
    for l in range(n_levels):
        blk = 1 << l
        prev_end = pltpu.roll(block_end, blk, axis=0)
        eq = b - jnp.where(tmod < blk, 0.0, prev_end)
        ek = block_end - b
        ql = (qq * jnp.exp(eq)).astype(BF16)
        kl = (kk * jnp.exp(ek)).astype(BF16)
        mask = ((diff_bits >> l) == 1) & (ti > si)
        for ch in range(n_chunks):
            sl = lax.dot_general(chunk_rows(ql, ch), chunk_rows(kl, ch), nt_dims,
                                 preferred_element_type=F32)
            scores[ch] = jnp.where(mask, sl, scores[ch])
        next_end = pltpu.roll(block_end, rows - blk, axis=0)
        block_end = jnp.where((tmod & blk) == 0, next_end, block_end)

    dsum = jnp.sum(qq * kk, axis=-1, keepdims=True)
    qe = (qq * jnp.exp(b)).astype(BF16)
    kd = (kk * jnp.exp(block_end - b)).astype(BF16)
    chunk_decay = jnp.exp(block_end)

    st = st_ref[...]
    outs = []
    for ch in range(n_chunks):
        sc = jnp.where(ti == si, chunk_rows(dsum, ch), scores[ch]).astype(BF16)
        v_c = chunk_rows(vb, ch)
        outs.append(jnp.dot(sc, v_c, preferred_element_type=F32)
                    + lax.dot_general(chunk_rows(qe, ch), st.astype(BF16), nt_dims,
                                      preferred_element_type=F32))
        st = (st * chunk_decay[ch * c:ch * c + 1]
              + lax.dot_general(v_c, chunk_rows(kd, ch), (((0,), (0,)), ((), ())),
                                preferred_element_type=F32))
    st_ref[...] = st

    out = jnp.concatenate(outs, axis=0)
    ms = jnp.mean(out * out, axis=-1, keepdims=True)
    o_ref[...] = (out * lax.rsqrt(ms + EPS) * og * jax.nn.silu(g_ref[...])).astype(o_ref.dtype)


def _hgrn_tril():
    c = HGRN_CHUNK
    return (jnp.arange(c)[None, :] <= jnp.arange(c)[:, None]).astype(BF16)


def hgrn_mixer(proj, batch, col0, lb_param, o_gain, layer):
    m = proj.shape[0]
    n_layers, width = lb_param.shape
    heads = width // HGRN_HEAD_DIM
    steps = m // batch // HGRN_ROWS
    hd = HGRN_HEAD_DIM
    cum = _hgrn_tril()

    def col(block):
        return pl.BlockSpec((HGRN_ROWS, hd),
                            lambda b, h, n: (b * steps + n, (col0 + block * width) // hd + h))

    return pl.pallas_call(
        functools.partial(_hgrn_kernel, layer=layer),
        grid=(batch, heads, steps),
        in_specs=[
            col(0), col(1), col(2), col(3),
            pl.BlockSpec((n_layers, hd), lambda b, h, n: (0, h)),
            pl.BlockSpec((1, hd), lambda b, h, n: (0, 0)),
            pl.BlockSpec(cum.shape, lambda b, h, n: (0, 0)),
        ],
        out_specs=pl.BlockSpec((HGRN_ROWS, hd), lambda b, h, n: (b * steps + n, h)),
        out_shape=jax.ShapeDtypeStruct((m, width), BF16),
        scratch_shapes=[pltpu.VMEM((hd, hd), F32)],
        compiler_params=_cparams("arbitrary", "arbitrary", "arbitrary"),
        name="hgrn_mixer",
    )(proj, proj, proj, proj, lb_param, o_gain.reshape(1, hd), cum)


def _attn_kernel(sink_ref, slope_ref, q_ref, kc_ref, vc_ref, kp_ref, vp_ref,
                 qg_ref, kg_ref, ind_ref, indt_ref, o_ref, bias_ref, *, n_kv):
    blk = ATT_BLOCK
    hd = ATT_HEAD_DIM
    nblk = pl.program_id(1)
    ind = ind_ref[...]
    indt = indt_ref[...]

    def head_rms_scale(x):
        w = x.shape[1]
        hi, lo = _split_bf16(x * x)
        ss = (jnp.dot(hi, ind[:w], preferred_element_type=F32)
              + jnp.dot(lo, ind[:w], preferred_element_type=F32))
        rs = lax.rsqrt(ss * (1.0 / hd) + EPS)
        hi, lo = _split_bf16(rs)
        return (jnp.dot(hi, indt[:, :w], preferred_element_type=F32)
                + jnp.dot(lo, indt[:, :w], preferred_element_type=F32))

    @pl.when(nblk <= 1)
    def _():
        t = lax.broadcasted_iota(jnp.int32, (blk, 2 * blk), 0)
        s = lax.broadcasted_iota(jnp.int32, (blk, 2 * blk), 1)
        dist = t + blk - s
        valid = (dist >= 0) & (dist < WINDOW) & ((s >= blk) | (nblk > 0))
        negdist = -dist.astype(F32)
        for h in range(bias_ref.shape[0]):
            tab = jnp.where(valid, (slope_ref[h] * LOG2E) * negdist, -jnp.inf)
            bias_ref[h] = jnp.where(s == 0, sink_ref[h] * LOG2E, tab)

    q = q_ref[...]
    qn = (q * head_rms_scale(q) * qg_ref[...] * (hd ** -0.5 * LOG2E)).astype(BF16)
    k = jnp.concatenate([kp_ref[...], kc_ref[...]], axis=0)
    kn = k * head_rms_scale(k) * kg_ref[...]
    v = jnp.concatenate([vp_ref[...], vc_ref[...]], axis=0)
    lane = lax.broadcasted_iota(jnp.int32, (2 * blk, LANES), 1)
    key_row = lax.broadcasted_iota(jnp.int32, (2 * blk, LANES), 0)
    low = lax.broadcasted_iota(jnp.int32, (blk, LANES), 1) < hd

    tiles = ATT_GROUP // 2
    for kv in range(n_kv):
        tile = kv // 2
        k_t = kn[:, tile * LANES:(tile + 1) * LANES]
        v_t = v[:, tile * LANES:(tile + 1) * LANES]
        own = ((lane < hd) if kv % 2 == 0 else (lane >= hd)) & (key_row > 0)
        k_own = jnp.where(own, k_t, 0.0)
        v_own = jnp.where(own, v_t, 0.0)
        k_half = [None, None]
        v_half = [None, None]
        k_half[kv % 2] = k_own
        v_half[kv % 2] = v_own
        k_half[1 - kv % 2] = pltpu.roll(k_own, hd, axis=1)
        v_half[1 - kv % 2] = pltpu.roll(v_own, hd, axis=1)
        for par in range(2):
            k_half[par] = k_half[par].astype(BF16)
            data_half = (lane < hd) if par == 0 else (lane >= hd)
            v_half[par] = jnp.where(data_half, v_half[par], 1.0).astype(BF16)
        for mt in range(tiles):
            qt = kv * tiles + mt
            q_t = qn[:, qt * LANES:(qt + 1) * LANES]
            pv = []
            for par in range(2):
                sc = lax.dot_general(q_t, k_half[par], (((1,), (1,)), ((), ())),
                                     preferred_element_type=F32) + bias_ref[2 * qt + par]
                mx = jnp.max(sc, axis=-1, keepdims=True)
                p = jnp.exp2(sc - mx).astype(BF16)
                pv.append(jnp.dot(p, v_half[par], preferred_element_type=F32))
            num = jnp.where(low, pv[0], pv[1])
            den = pltpu.roll(jnp.where(low, pv[1], pv[0]), hd, axis=1)
            o_ref[:, qt * LANES:(qt + 1) * LANES] = (num / den).astype(o_ref.dtype)


def attention(qkv, batch, n_heads, n_kv, q_gain, k_gain, sinks):
    m = qkv.shape[0]
    hd = ATT_HEAD_DIM
    qw = n_heads * hd
    kw = n_kv * hd
    steps = m // batch // ATT_BLOCK
    slopes = jnp.exp2(-8.0 * jnp.arange(1, n_heads + 1, dtype=F32) / n_heads)
    head_of_lane = jnp.arange(qw) // hd
    ind = (head_of_lane[:, None] == jnp.arange(LANES)[None, :]).astype(BF16)
    cur = lambda b, n, *_: b * steps + n
    prev = lambda b, n, *_: b * steps + jnp.maximum(n - 1, 0)
    grid_spec = pltpu.PrefetchScalarGridSpec(
        num_scalar_prefetch=2,
        grid=(batch, steps),
        in_specs=[
            pl.BlockSpec((ATT_BLOCK, qw), lambda b, n, *_: (cur(b, n), 0)),
            pl.BlockSpec((ATT_BLOCK, kw), lambda b, n, *_: (cur(b, n), qw // kw)),
            pl.BlockSpec((ATT_BLOCK, kw), lambda b, n, *_: (cur(b, n), qw // kw + 1)),
            pl.BlockSpec((ATT_BLOCK, kw), lambda b, n, *_: (prev(b, n), qw // kw)),
            pl.BlockSpec((ATT_BLOCK, kw), lambda b, n, *_: (prev(b, n), qw // kw + 1)),
            pl.BlockSpec((1, qw), lambda b, n, *_: (0, 0)),
            pl.BlockSpec((1, kw), lambda b, n, *_: (0, 0)),
            pl.BlockSpec(ind.shape, lambda b, n, *_: (0, 0)),
            pl.BlockSpec(ind.shape[::-1], lambda b, n, *_: (0, 0)),
        ],
        out_specs=pl.BlockSpec((ATT_BLOCK, qw), lambda b, n, *_: (cur(b, n), 0)),
        scratch_shapes=[pltpu.VMEM((n_heads, ATT_BLOCK, 2 * ATT_BLOCK), F32)],
    )
    return pl.pallas_call(
        functools.partial(_attn_kernel, n_kv=n_kv),
        grid_spec=grid_spec,
        out_shape=jax.ShapeDtypeStruct((m, qw), BF16),
        compiler_params=_cparams("arbitrary", "arbitrary"),
        name="swa_attention",
    )(sinks, slopes, qkv, qkv, qkv, qkv, qkv,
      jnp.tile(q_gain, n_heads).reshape(1, qw), jnp.tile(k_gain, n_kv).reshape(1, kw),
      ind, ind.T)


def kernel(x, even_norm, even_w_in, s5_lambda_re, s5_lambda_im, s5_log_dt, s5_b_re, s5_b_im,
           s5_c_re, s5_c_im, s5_d, s5_w_glu, s5_b_glu, hgrn_lower_bound, hgrn_o_norm, even_w_out,
           odd_norm, odd_w_qkv, q_norm, k_norm, att_sinks, odd_w_out, mlp_norm, mlp_w_up, mlp_w_down):
    batch, seqlen, d_model = x.shape
    depth = mlp_norm.shape[0]
    s5_width = s5_w_glu.shape[1]
    n_heads = att_sinks.shape[1]
    n_kv = n_heads // ATT_GROUP
    assert seqlen % max(S5_ROWS, HGRN_ROWS, ATT_BLOCK) == 0
    h = x.reshape(batch * seqlen, d_model)
    for layer in range(depth):
        j = layer // 2
        if layer % 2 == 0:
            proj = norm_matmul(h, even_norm[j], even_w_in, j, tn=512)
            prep = s5_prep(s5_lambda_re[j], s5_lambda_im[j], s5_log_dt[j], s5_b_re[j], s5_b_im[j])
            params = _s5_layouts(*prep, s5_c_re[j], s5_c_im[j])
            y_a = s5_mixer(proj, batch, params, s5_d[j], s5_w_glu[j], s5_b_glu[j])
            y_b = hgrn_mixer(proj, batch, s5_width, hgrn_lower_bound, hgrn_o_norm[j], j)
            h = proj_residual([y_a, y_b], even_w_out, j, h, tn=512)
        else:
            qkv = norm_matmul(h, odd_norm[j], odd_w_qkv, j, tn=512)
            o = attention(qkv, batch, n_heads, n_kv, q_norm[j], k_norm[j], att_sinks[j])
            h = proj_residual([o], odd_w_out, j, h, tn=512)
        h = mlp_residual(h, mlp_norm[layer], mlp_w_up, mlp_w_down, layer, tf=512)
    return h.reshape(batch, seqlen, d_model)
```

```python
import functools
import math

import jax
import jax.numpy as jnp
from jax import lax
from jax.experimental import pallas as pl
from jax.experimental.pallas import tpu as pltpu

F32 = jnp.float32
BF16 = jnp.bfloat16
EPS = 1e-6
LOG2E = math.log2(math.e)

LANES = 128
SUBLANES = 8
VMEM_LIMIT = 56 * 1024 * 1024

S5_GROUP_SIZE = 16
S5_STATE = 64
S5_MIN_DECAY = 1e-4
HGRN_HEAD_DIM = 128
HGRN_CHUNK = 64
ATT_HEAD_DIM = 64
ATT_GROUP = 8
WINDOW = 128
ATT_BLOCK = 128

PROJ_TILE = (2048, 256)
OUT_TILE = (2048, 512)
MLP_TILE = (1024, 512)
NORM_ROWS = 128
S5_ROWS = 256
S5_SEGS = SUBLANES
S5_SEG = S5_ROWS // S5_SEGS
S5_PITCH = S5_SEG + 8
S5_TILE_GROUPS = LANES // S5_GROUP_SIZE
S5_TILE_STATES = S5_TILE_GROUPS * S5_STATE
HGRN_ROWS = 512


def _cparams(*sem):
    return pltpu.CompilerParams(dimension_semantics=sem, vmem_limit_bytes=VMEM_LIMIT)


def _split_bf16(x):
    hi = x.astype(BF16)
    lo = (x - hi.astype(F32)).astype(BF16)
    return hi, lo


def _rmsnorm_rows(h_ref, g_ref, xn_ref):
    rows = h_ref.shape[0]
    g = g_ref[...]

    def body(i, carry):
        r = pl.multiple_of(i * NORM_ROWS, NORM_ROWS)
        x = h_ref[pl.ds(r, NORM_ROWS), :]
        ms = jnp.mean(x * x, axis=-1, keepdims=True)
        xn_ref[pl.ds(r, NORM_ROWS), :] = (x * lax.rsqrt(ms + EPS) * g).astype(BF16)
        return carry

    lax.fori_loop(0, rows // NORM_ROWS, body, 0)


def _norm_matmul_kernel(h_ref, g_ref, w_ref, o_ref, xn_ref):
    @pl.when(pl.program_id(1) == 0)
    def _():
        _rmsnorm_rows(h_ref, g_ref, xn_ref)

    o_ref[...] = jnp.dot(xn_ref[...], w_ref[...].astype(BF16), preferred_element_type=F32)


def norm_matmul(h, gain, w_stack, layer, tm, tn):
    m, k = h.shape
    n = w_stack.shape[2]
    tm = min(tm, m)
    return pl.pallas_call(
        _norm_matmul_kernel,
        grid=(m // tm, n // tn),
        in_specs=[
            pl.BlockSpec((tm, k), lambda i, j: (i, 0)),
            pl.BlockSpec((1, k), lambda i, j: (0, 0)),
            pl.BlockSpec((None, k, tn), lambda i, j: (layer, 0, j)),
        ],
        out_specs=pl.BlockSpec((tm, tn), lambda i, j: (i, j)),
        out_shape=jax.ShapeDtypeStruct((m, n), F32),
        scratch_shapes=[pltpu.VMEM((tm, k), BF16)],
        compiler_params=_cparams("parallel", "arbitrary"),
        name="norm_matmul",
    )(h, gain.reshape(1, k), w_stack)


def _proj_residual_kernel(*refs, n_lhs):
    lhs_refs = refs[:n_lhs]
    w_ref, res_ref, o_ref = refs[n_lhs:]
    acc = res_ref[...]
    k0 = 0
    for lhs_ref in lhs_refs:
        kk = lhs_ref.shape[1]
        acc = acc + jnp.dot(lhs_ref[...], w_ref[k0:k0 + kk, :].astype(BF16),
                            preferred_element_type=F32)
        k0 += kk
    o_ref[...] = acc


def proj_residual(lhs_list, w_stack, layer, res, tm, tn):
    m, n = res.shape
    k = w_stack.shape[1]
    tm = min(tm, m)
    in_specs = [pl.BlockSpec((tm, x.shape[1]), lambda i, j: (i, 0)) for x in lhs_list]
    in_specs += [pl.BlockSpec((None, k, tn), lambda i, j: (layer, 0, j)),
                 pl.BlockSpec((tm, tn), lambda i, j: (i, j))]
    return pl.pallas_call(
        functools.partial(_proj_residual_kernel, n_lhs=len(lhs_list)),
        grid=(m // tm, n // tn),
        in_specs=in_specs,
        out_specs=pl.BlockSpec((tm, tn), lambda i, j: (i, j)),
        out_shape=jax.ShapeDtypeStruct((m, n), F32),
        compiler_params=_cparams("parallel", "arbitrary"),
        name="proj_residual",
    )(*lhs_list, w_stack, res)


def _mlp_kernel(h_hbm, g_ref, wu_ref, wd_ref, o_ref, xn_ref, h_buf, h_sem):
    i = pl.program_id(0)
    f = pl.program_id(1)
    tm = h_buf.shape[0]

    def h_copy(tile):
        return pltpu.make_async_copy(h_hbm.at[pl.ds(tile * tm, tm), :], h_buf, h_sem)

    @pl.when((i == 0) & (f == 0))
    def _():
        h_copy(0).start()

    @pl.when(f == 0)
    def _():
        h_copy(i).wait()
        _rmsnorm_rows(h_buf, g_ref, xn_ref)
        o_ref[...] = h_buf[...]

    @pl.when((f == 1) & (i + 1 < pl.num_programs(0)))
    def _():
        h_copy(i + 1).start()

    up = jnp.dot(xn_ref[...], wu_ref[...].astype(BF16), preferred_element_type=F32)
    act = jnp.square(jnp.maximum(up, 0.0)).astype(BF16)
    o_ref[...] += jnp.dot(act, wd_ref[...].astype(BF16), preferred_element_type=F32)


def mlp_residual(h, gain, w_up_stack, w_down_stack, layer, tm, tf):
    m, d = h.shape
    f = w_up_stack.shape[2]
    tm = min(tm, m)
    assert f // tf >= 2
    return pl.pallas_call(
        _mlp_kernel,
        grid=(m // tm, f // tf),
        in_specs=[
            pl.BlockSpec(memory_space=pl.ANY),
            pl.BlockSpec((1, d), lambda i, j: (0, 0)),
            pl.BlockSpec((None, d, tf), lambda i, j: (layer, 0, j)),
            pl.BlockSpec((None, tf, d), lambda i, j: (layer, j, 0)),
        ],
        out_specs=pl.BlockSpec((tm, d), lambda i, j: (i, 0)),
        out_shape=jax.ShapeDtypeStruct((m, d), F32),
        scratch_shapes=[pltpu.VMEM((tm, d), BF16), pltpu.VMEM((tm, d), F32),
                        pltpu.SemaphoreType.DMA(())],
        compiler_params=_cparams("arbitrary", "arbitrary"),
        name="mlp_residual",
    )(h, gain.reshape(1, d), w_up_stack, w_down_stack)


def _s5_prep_kernel(lre_ref, lim_ref, ldt_ref, bre_ref, bim_ref,
                    ar_ref, ai_ref, asr_ref, asi_ref, bbr_ref, bbi_ref):
    lr = jnp.minimum(lre_ref[...], -S5_MIN_DECAY)
    li = lim_ref[...]
    dt = jnp.exp(ldt_ref[...])
    mag = jnp.exp(lr * dt)
    ar = mag * jnp.cos(li * dt)
    ai = mag * jnp.sin(li * dt)
    den = lr * lr + li * li
    zr = ((ar - 1.0) * lr + ai * li) / den
    zi = (ai * lr - (ar - 1.0) * li) / den
    ar_ref[...] = ar
    ai_ref[...] = ai
    pr, pi = ar, ai
    for _ in range(int(math.log2(S5_SEG))):
        pr, pi = pr * pr - pi * pi, 2.0 * pr * pi
    asr_ref[...] = pr
    asi_ref[...] = pi
    for c in range(S5_GROUP_SIZE):
        br = bre_ref[c]
        bi = bim_ref[c]
        bbr_ref[c] = zr * br - zi * bi
        bbi_ref[c] = zr * bi + zi * br


def s5_prep(lam_re, lam_im, log_dt, b_re, b_im):
    g, p = lam_re.shape
    ldt = jnp.broadcast_to(log_dt[:, None], (g, p))
    bre_t = jnp.transpose(b_re, (2, 0, 1))
    bim_t = jnp.transpose(b_im, (2, 0, 1))
    gp = jax.ShapeDtypeStruct((g, p), F32)
    cgp = jax.ShapeDtypeStruct((S5_GROUP_SIZE, g, p), F32)
    return pl.pallas_call(
        _s5_prep_kernel,
        out_shape=(gp, gp, gp, gp, cgp, cgp),
        name="s5_prep",
    )(lam_re, lam_im, ldt, bre_t, bim_t)


def _s5_kernel(u_ref, bblk_ref, ca_ref, cb_ref, ar_ref, ai_ref, asr_ref, asi_ref,
               d_ref, wglu_ref, bglu_ref, o_ref,
               bu_ref, xp_ref, yp_ref, yn_ref, st_ref):
    n_tiles = u_ref.shape[1] // LANES
    half = S5_TILE_STATES // LANES
    slabs = 2 * half

    @pl.when(pl.program_id(1) == 0)
    def _():
        st_ref[...] = jnp.zeros_like(st_ref)

    u = u_ref[...]
    ub = u.astype(BF16)
    for j in range(n_tiles):
        res = jnp.dot(ub[:, j * LANES:(j + 1) * LANES], bblk_ref[j], preferred_element_type=F32)
        for seg in range(S5_SEGS):
            for k in range(slabs):
                bu_ref[j * slabs + k, seg * S5_PITCH:seg * S5_PITCH + S5_SEG, :] = (
                    res[seg * S5_SEG:(seg + 1) * S5_SEG, k * LANES:(k + 1) * LANES])

    row = lax.broadcasted_iota(jnp.int32, (SUBLANES, LANES), 0)

    def tile_scan(j, carry):
        a_r = [ar_ref[j, :, k * LANES:(k + 1) * LANES] for k in range(half)]
        a_i = [ai_ref[j, :, k * LANES:(k + 1) * LANES] for k in range(half)]

        def load(s, k):
            return bu_ref[j * slabs + k, pl.ds(s, S5_SEGS, stride=S5_PITCH), :]

        def step1(s, x):
            xr, xi = x
            nr, ni = [], []
            for k in range(half):
                br = load(s, k)
                bi = load(s, half + k)
                nr.append(a_r[k] * xr[k] - a_i[k] * xi[k] + br)
                ni.append(a_r[k] * xi[k] + a_i[k] * xr[k] + bi)
            return tuple(nr), tuple(ni)

        zero = tuple(jnp.zeros((SUBLANES, LANES), F32) for _ in range(half))
        fr, fi = lax.fori_loop(0, S5_SEG, step1, (zero, zero), unroll=4)

        init_r, init_i, new_r, new_i = [], [], [], []
        for k in range(half):
            sr = asr_ref[j, :, k * LANES:(k + 1) * LANES]
            si = asi_ref[j, :, k * LANES:(k + 1) * LANES]
            pr = st_ref[j, :, k * LANES:(k + 1) * LANES]
            pi = st_ref[j, :, (half + k) * LANES:(half + k + 1) * LANES]
            ir = jnp.where(row == 0, pr, 0.0)
            ii = jnp.where(row == 0, pi, 0.0)
            for seg in range(1, S5_SEGS):
                er = fr[k] + sr * ir - si * ii
                ei = fi[k] + sr * ii + si * ir
                ir = jnp.where(row == seg, pltpu.roll(er, 1, axis=0), ir)
                ii = jnp.where(row == seg, pltpu.roll(ei, 1, axis=0), ii)
            er = fr[k] + sr * ir - si * ii
            ei = fi[k] + sr * ii + si * ir
            init_r.append(ir)
            init_i.append(ii)
            new_r.append(pltpu.roll(er, 1, axis=0))
            new_i.append(pltpu.roll(ei, 1, axis=0))
        for k in range(half):
            st_ref[j, :, k * LANES:(k + 1) * LANES] = new_r[k]
            st_ref[j, :, (half + k) * LANES:(half + k + 1) * LANES] = new_i[k]

        def step2(s2, x):
            xr, xn = x
            outs_r, outs_n = [], []
            for _ in range(2):
                outs_r.append([])
                outs_n.append([])
            for t in range(2):
                s = s2 * 2 + t
                nr, nn = [], []
                for k in range(half):
                    br = load(s, k)
                    bi = load(s, half + k)
                    nr.append(a_r[k] * xr[k] + a_i[k] * xn[k] + br)
                    nn.append(a_r[k] * xn[k] - a_i[k] * xr[k] - bi)
                xr, xn = tuple(nr), tuple(nn)
                outs_r[t] = nr
                outs_n[t] = nn
            r0 = pl.multiple_of(s2 * (2 * S5_SEGS), 2 * S5_SEGS)
            for k in range(half):
                xp_ref[j, pl.ds(r0, 2 * S5_SEGS), k * LANES:(k + 1) * LANES] = (
                    jnp.concatenate([outs_r[0][k], outs_r[1][k]], axis=0).astype(BF16))
                xp_ref[j, pl.ds(r0, 2 * S5_SEGS), (half + k) * LANES:(half + k + 1) * LANES] = (
                    jnp.concatenate([outs_n[0][k], outs_n[1][k]], axis=0).astype(BF16))
            return xr, xn

        neg_i = tuple(-v for v in init_i)
        lax.fori_loop(0, S5_SEG // 2, step2, (tuple(init_r), neg_i), unroll=2)
        return carry

    lax.fori_loop(0, n_tiles, tile_scan, 0)

    for jj in range(n_tiles // 2):
        yy = (jnp.dot(xp_ref[2 * jj], ca_ref[jj], preferred_element_type=F32)
              + jnp.dot(xp_ref[2 * jj + 1], cb_ref[jj], preferred_element_type=F32))
        yp_ref[2 * jj] = yy[:, :LANES]
        yp_ref[2 * jj + 1] = yy[:, LANES:]

    for j in range(n_tiles):
        for seg in range(S5_SEGS):
            yn_ref[seg * S5_SEG:(seg + 1) * S5_SEG, j * LANES:(j + 1) * LANES] = (
                yp_ref[j, pl.ds(seg, S5_SEG, stride=S5_SEGS), :])

    y = yn_ref[...] + d_ref[...] * u
    y = jax.nn.gelu(y)
    z = jnp.dot(y.astype(BF16), wglu_ref[...], preferred_element_type=F32) + bglu_ref[...]
    o_ref[...] = (y * jax.nn.sigmoid(z)).astype(o_ref.dtype)


def _s5_layouts(ar, ai, asr, asi, bbr, bbi, c_re, c_im):
    g, p = ar.shape
    tg = S5_TILE_GROUPS
    nt = g // tg
    eye = jnp.eye(tg, dtype=F32)

    def lanes(a):
        return jnp.broadcast_to(a.reshape(nt, 1, tg * p), (nt, SUBLANES, tg * p))

    bb = jnp.stack([bbr, bbi]).reshape(2, S5_GROUP_SIZE, nt, tg, p)
    bb = jnp.transpose(bb, (2, 3, 1, 0, 4))
    bblk = bb[:, :, :, :, None, :] * eye[None, :, None, None, :, None]
    bblk = bblk.reshape(nt, tg * S5_GROUP_SIZE, 2 * tg * p).astype(BF16)

    cc = jnp.stack([c_re, c_im]).reshape(2, nt, tg, S5_GROUP_SIZE, p)
    cc = jnp.transpose(cc, (1, 0, 2, 4, 3))
    cblk = cc[:, :, :, :, None, :] * eye[None, None, :, None, :, None]
    cblk = cblk.reshape(nt, 2 * tg * p, tg * S5_GROUP_SIZE)
    zeros = jnp.zeros_like(cblk[0::2])
    ca = jnp.concatenate([cblk[0::2], zeros], axis=-1).astype(BF16)
    cb = jnp.concatenate([zeros, cblk[1::2]], axis=-1).astype(BF16)
    return bblk, ca, cb, lanes(ar), lanes(ai), lanes(asr), lanes(asi)


def s5_mixer(proj, batch, params, d_skip, w_glu, b_glu):
    bblk, ca, cb, ar, ai, asr, asi = params
    m = proj.shape[0]
    width = w_glu.shape[0]
    nt = width // LANES
    steps = m // batch // S5_ROWS
    slabs = 2 * S5_TILE_STATES // LANES
    const3 = lambda b, n: (0, 0, 0)
    const2 = lambda b, n: (0, 0)
    return pl.pallas_call(
        _s5_kernel,
        grid=(batch, steps),
        in_specs=[
            pl.BlockSpec((S5_ROWS, width), lambda b, n: (b * steps + n, 0)),
            pl.BlockSpec(bblk.shape, const3),
            pl.BlockSpec(ca.shape, const3),
            pl.BlockSpec(cb.shape, const3),
            pl.BlockSpec(ar.shape, const3),
            pl.BlockSpec(ai.shape, const3),
            pl.BlockSpec(asr.shape, const3),
            pl.BlockSpec(asi.shape, const3),
            pl.BlockSpec((1, width), const2),
            pl.BlockSpec((width, width), const2),
            pl.BlockSpec((1, width), const2),
        ],
        out_specs=pl.BlockSpec((S5_ROWS, width), lambda b, n: (b * steps + n, 0)),
        out_shape=jax.ShapeDtypeStruct((m, width), BF16),
        scratch_shapes=[
            pltpu.VMEM((nt * slabs, S5_SEGS * S5_PITCH, LANES), F32),
            pltpu.VMEM((nt, S5_ROWS, 2 * S5_TILE_STATES), BF16),
            pltpu.VMEM((nt, S5_ROWS, LANES), F32),
            pltpu.VMEM((S5_ROWS, width), F32),
            pltpu.VMEM((nt, SUBLANES, 2 * S5_TILE_STATES), F32),
        ],
        compiler_params=_cparams("arbitrary", "arbitrary"),
        name="s5_mixer",
    )(proj, bblk, ca, cb, ar, ai, asr, asi, d_skip.reshape(1, width),
      w_glu.astype(BF16), b_glu.reshape(1, width))


def _hgrn_kernel(q_ref, f_ref, i_ref, g_ref, lbp_ref, og_ref, cum_ref, o_ref, st_ref, *, layer):
    n_levels = int(math.log2(HGRN_CHUNK))
    c = HGRN_CHUNK
    hd = HGRN_HEAD_DIM

    @pl.when(pl.program_id(2) == 0)
    def _():
        st_ref[...] = jnp.zeros_like(st_ref)

    lbp = lbp_ref[...]
    e = jnp.exp(lbp - jnp.max(lbp, axis=0, keepdims=True))
    prob = e / jnp.sum(e, axis=0, keepdims=True)
    lb = jnp.sum(prob[:layer + 1], axis=0, keepdims=True) - prob[0:1]
    og = og_ref[...]

    rows = q_ref.shape[0]
    n_chunks = rows // c
    ti = lax.broadcasted_iota(jnp.int32, (c, c), 0)
    si = lax.broadcasted_iota(jnp.int32, (c, c), 1)
    diff_bits = ti ^ si
    tril = cum_ref[...]
    tmod = lax.broadcasted_iota(jnp.int32, (rows, hd), 0) & (c - 1)
    nt_dims = (((1,), (1,)), ((), ()))

    def chunk_rows(x, ch):
        return x[ch * c:(ch + 1) * c]

    forget = lb + (1.0 - lb) * jax.nn.sigmoid(f_ref[...])
    logf = jnp.log2(forget)
    kk = 1.0 - forget
    qq = jax.nn.silu(q_ref[...])
    vb = i_ref[...].astype(BF16)

    hi, lo = _split_bf16(logf)
    b = jnp.concatenate(
        [jnp.dot(tril, chunk_rows(hi, ch), preferred_element_type=F32)
         + jnp.dot(tril, chunk_rows(lo, ch), preferred_element_type=F32) for ch in range(n_chunks)],
        axis=0)

    block_end = b
    scores = [jnp.zeros((c, c), F32) for _ in range(n_chunks)]
    for l in range(n_levels):
        blk = 1 << l
        prev_end = pltpu.roll(block_end, blk, axis=0)
        eq = b - jnp.where(tmod < blk, 0.0, prev_end)
        ek = block_end - b
        ql = (qq * jnp.exp2(eq)).astype(BF16)
        kl = (kk * jnp.exp2(ek)).astype(BF16)
        mask = ((diff_bits >> l) == 1) & (ti > si)
        for ch in range(n_chunks):
            sl = lax.dot_general(chunk_rows(ql, ch), chunk_rows(kl, ch), nt_dims,
                                 preferred_element_type=F32)
            scores[ch] = jnp.where(mask, sl, scores[ch])
        next_end = pltpu.roll(block_end, rows - blk, axis=0)
        block_end = jnp.where((tmod & blk) == 0, next_end, block_end)

    dsum = jnp.sum(qq * kk, axis=-1, keepdims=True)
    qe = (qq * jnp.exp2(b)).astype(BF16)
    kd = (kk * jnp.exp2(block_end - b)).astype(BF16)
    chunk_decay = jnp.exp2(block_end)

    st = st_ref[...]
    outs = []
    for ch in range(n_chunks):
        sc = jnp.where(ti == si, chunk_rows(dsum, ch), scores[ch]).astype(BF16)
        v_c = chunk_rows(vb, ch)
        outs.append(jnp.dot(sc, v_c, preferred_element_type=F32)
                    + lax.dot_general(chunk_rows(qe, ch), st.astype(BF16), nt_dims,
                                      preferred_element_type=F32))
        st = (st * chunk_decay[ch * c:ch * c + 1]
              + lax.dot_general(v_c, chunk_rows(kd, ch), (((0,), (0,)), ((), ())),
                                preferred_element_type=F32))
    st_ref[...] = st

    out = jnp.concatenate(outs, axis=0)
    ms = jnp.mean(out * out, axis=-1, keepdims=True)
    o_ref[...] = (out * lax.rsqrt(ms + EPS) * og * jax.nn.silu(g_ref[...])).astype(o_ref.dtype)


def _hgrn_tril():
    c = HGRN_CHUNK
    return (jnp.arange(c)[None, :] <= jnp.arange(c)[:, None]).astype(BF16)


def hgrn_mixer(proj, batch, col0, lb_param, o_gain, layer):
    m = proj.shape[0]
    n_layers, width = lb_param.shape
    heads = width // HGRN_HEAD_DIM
    steps = m // batch // HGRN_ROWS
    hd = HGRN_HEAD_DIM
    cum = _hgrn_tril()

    def col(block):
        return pl.BlockSpec((HGRN_ROWS, hd),
                            lambda b, h, n: (b * steps + n, (col0 + block * width) // hd + h))

    return pl.pallas_call(
        functools.partial(_hgrn_kernel, layer=layer),
        grid=(batch, heads, steps),
        in_specs=[
            col(0), col(1), col(2), col(3),
            pl.BlockSpec((n_layers, hd), lambda b, h, n: (0, h)),
            pl.BlockSpec((1, hd), lambda b, h, n: (0, 0)),
            pl.BlockSpec(cum.shape, lambda b, h, n: (0, 0)),
        ],
        out_specs=pl.BlockSpec((HGRN_ROWS, hd), lambda b, h, n: (b * steps + n, h)),
        out_shape=jax.ShapeDtypeStruct((m, width), BF16),
        scratch_shapes=[pltpu.VMEM((hd, hd), F32)],
        compiler_params=_cparams("arbitrary", "arbitrary", "arbitrary"),
        name="hgrn_mixer",
    )(proj, proj, proj, proj, lb_param, o_gain.reshape(1, hd), cum)


def _attn_kernel(sink_ref, slope_ref, q_ref, kc_ref, vc_ref, kp_ref, vp_ref,
                 qg_ref, kg_ref, ind_ref, indt_ref, o_ref, bias_ref, *, n_kv):
    blk = ATT_BLOCK
    hd = ATT_HEAD_DIM
    nblk = pl.program_id(1)
    ind = ind_ref[...]
    indt = indt_ref[...]

    def head_rms_scale(x):
        w = x.shape[1]
        hi, lo = _split_bf16(x * x)
        ss = (jnp.dot(hi, ind[:w], preferred_element_type=F32)
              + jnp.dot(lo, ind[:w], preferred_element_type=F32))
        rs = lax.rsqrt(ss * (1.0 / hd) + EPS)
        hi, lo = _split_bf16(rs)
        return (jnp.dot(hi, indt[:, :w], preferred_element_type=F32)
                + jnp.dot(lo, indt[:, :w], preferred_element_type=F32))

    @pl.when(nblk <= 1)
    def _():
        t = lax.broadcasted_iota(jnp.int32, (blk, 2 * blk), 0)
        s = lax.broadcasted_iota(jnp.int32, (blk, 2 * blk), 1)
        dist = t + blk - s
        valid = (dist >= 0) & (dist < WINDOW) & ((s >= blk) | (nblk > 0))
        negdist = -dist.astype(F32)
        for h in range(bias_ref.shape[0]):
            tab = jnp.where(valid, (slope_ref[h] * LOG2E) * negdist, -jnp.inf)
            bias_ref[h] = jnp.where(s == 0, sink_ref[h] * LOG2E, tab)

    q = q_ref[...]
    qn = (q * head_rms_scale(q) * qg_ref[...] * (hd ** -0.5 * LOG2E)).astype(BF16)
    k = jnp.concatenate([kp_ref[...], kc_ref[...]], axis=0)
    kn = k * head_rms_scale(k) * kg_ref[...]
    v = jnp.concatenate([vp_ref[...], vc_ref[...]], axis=0)
    lane = lax.broadcasted_iota(jnp.int32, (2 * blk, LANES), 1)
    key_row = lax.broadcasted_iota(jnp.int32, (2 * blk, LANES), 0)
    low = lax.broadcasted_iota(jnp.int32, (blk, LANES), 1) < hd

    tiles = ATT_GROUP // 2
    for kv in range(n_kv):
        tile = kv // 2
        k_t = kn[:, tile * LANES:(tile + 1) * LANES]
        v_t = v[:, tile * LANES:(tile + 1) * LANES]
        own = ((lane < hd) if kv % 2 == 0 else (lane >= hd)) & (key_row > 0)
        k_own = jnp.where(own, k_t, 0.0)
        v_own = jnp.where(own, v_t, 0.0)
        k_half = [None, None]
        v_half = [None, None]
        k_half[kv % 2] = k_own
        v_half[kv % 2] = v_own
        k_half[1 - kv % 2] = pltpu.roll(k_own, hd, axis=1)
        v_half[1 - kv % 2] = pltpu.roll(v_own, hd, axis=1)
        for par in range(2):
            k_half[par] = k_half[par].astype(BF16)
            data_half = (lane < hd) if par == 0 else (lane >= hd)
            v_half[par] = jnp.where(data_half, v_half[par], 1.0).astype(BF16)
        for mt in range(tiles):
            qt = kv * tiles + mt
            q_t = qn[:, qt * LANES:(qt + 1) * LANES]
            pv = []
            for par in range(2):
                sc = lax.dot_general(q_t, k_half[par], (((1,), (1,)), ((), ())),
                                     preferred_element_type=F32) + bias_ref[2 * qt + par]
                mx = jnp.max(sc, axis=-1, keepdims=True)
                p = jnp.exp2(sc - mx).astype(BF16)
                pv.append(jnp.dot(p, v_half[par], preferred_element_type=F32))
            num = jnp.where(low, pv[0], pv[1])
            den = pltpu.roll(jnp.where(low, pv[1], pv[0]), hd, axis=1)
            o_ref[:, qt * LANES:(qt + 1) * LANES] = (num / den).astype(o_ref.dtype)


def attention(qkv, batch, n_heads, n_kv, q_gain, k_gain, sinks):
    m = qkv.shape[0]
    hd = ATT_HEAD_DIM
    qw = n_heads * hd
    kw = n_kv * hd
    steps = m // batch // ATT_BLOCK
    slopes = jnp.exp2(-8.0 * jnp.arange(1, n_heads + 1, dtype=F32) / n_heads)
    head_of_lane = jnp.arange(qw) // hd
    ind = (head_of_lane[:, None] == jnp.arange(LANES)[None, :]).astype(BF16)
    cur = lambda b, n, *_: b * steps + n
    prev = lambda b, n, *_: b * steps + jnp.maximum(n - 1, 0)
    grid_spec = pltpu.PrefetchScalarGridSpec(
        num_scalar_prefetch=2,
        grid=(batch, steps),
        in_specs=[
            pl.BlockSpec((ATT_BLOCK, qw), lambda b, n, *_: (cur(b, n), 0)),
            pl.BlockSpec((ATT_BLOCK, kw), lambda b, n, *_: (cur(b, n), qw // kw)),
            pl.BlockSpec((ATT_BLOCK, kw), lambda b, n, *_: (cur(b, n), qw // kw + 1)),
            pl.BlockSpec((ATT_BLOCK, kw), lambda b, n, *_: (prev(b, n), qw // kw)),
            pl.BlockSpec((ATT_BLOCK, kw), lambda b, n, *_: (prev(b, n), qw // kw + 1)),
            pl.BlockSpec((1, qw), lambda b, n, *_: (0, 0)),
            pl.BlockSpec((1, kw), lambda b, n, *_: (0, 0)),
            pl.BlockSpec(ind.shape, lambda b, n, *_: (0, 0)),
            pl.BlockSpec(ind.shape[::-1], lambda b, n, *_: (0, 0)),
        ],
        out_specs=pl.BlockSpec((ATT_BLOCK, qw), lambda b, n, *_: (cur(b, n), 0)),
        scratch_shapes=[pltpu.VMEM((n_heads, ATT_BLOCK, 2 * ATT_BLOCK), F32)],
    )
    return pl.pallas_call(
        functools.partial(_attn_kernel, n_kv=n_kv),
        grid_spec=grid_spec,
        out_shape=jax.ShapeDtypeStruct((m, qw), BF16),
        compiler_params=_cparams("arbitrary", "arbitrary"),
        name="swa_attention",
    )(sinks, slopes, qkv, qkv, qkv, qkv, qkv,
      jnp.tile(q_gain, n_heads).reshape(1, qw), jnp.tile(k_gain, n_kv).reshape(1, kw),
      ind, ind.T)


def kernel(x, even_norm, even_w_in, s5_lambda_re, s5_lambda_im, s5_log_dt, s5_b_re, s5_b_im,
           s5_c_re, s5_c_im, s5_d, s5_w_glu, s5_b_glu, hgrn_lower_bound, hgrn_o_norm, even_w_out,
           odd_norm, odd_w_qkv, q_norm, k_norm, att_sinks, odd_w_out, mlp_norm, mlp_w_up, mlp_w_down):
    batch, seqlen, d_model = x.shape
    depth = mlp_norm.shape[0]
    s5_width = s5_w_glu.shape[1]
    n_heads = att_sinks.shape[1]
    n_kv = n_heads // ATT_GROUP
    assert seqlen % max(S5_ROWS, HGRN_ROWS, ATT_BLOCK) == 0
    h = x.reshape(batch * seqlen, d_model)
    for layer in range(depth):
        j = layer // 2
        if layer % 2 == 0:
            proj = norm_matmul(h, even_norm[j], even_w_in, j, *PROJ_TILE)
            prep = s5_prep(s5_lambda_re[j], s5_lambda_im[j], s5_log_dt[j], s5_b_re[j], s5_b_im[j])
            params = _s5_layouts(*prep, s5_c_re[j], s5_c_im[j])
            y_a = s5_mixer(proj, batch, params, s5_d[j], s5_w_glu[j], s5_b_glu[j])
            y_b = hgrn_mixer(proj, batch, s5_width, hgrn_lower_bound, hgrn_o_norm[j], j)
            h = proj_residual([y_a, y_b], even_w_out, j, h, *OUT_TILE)
        else:
            qkv = norm_matmul(h, odd_norm[j], odd_w_qkv, j, *PROJ_TILE)
            o = attention(qkv, batch, n_heads, n_kv, q_norm[j], k_norm[j], att_sinks[j])
            h = proj_residual([o], odd_w_out, j, h, *OUT_TILE)
        h = mlp_residual(h, mlp_norm[layer], mlp_w_up, mlp_w_down, layer, *MLP_TILE)
    return h.reshape(batch, seqlen, d_model)
```

```python
import functools
import math

import jax
import jax.numpy as jnp
from jax import lax
from jax.experimental import pallas as pl
from jax.experimental.pallas import tpu as pltpu

F32 = jnp.float32
BF16 = jnp.bfloat16
EPS = 1e-6
LOG2E = math.log2(math.e)

LANES = 128
SUBLANES = 8
VMEM_LIMIT = 56 * 1024 * 1024

S5_GROUP_SIZE = 16
S5_STATE = 64
S5_MIN_DECAY = 1e-4
HGRN_HEAD_DIM = 128
HGRN_CHUNK = 64
ATT_HEAD_DIM = 64
ATT_GROUP = 8
WINDOW = 128
ATT_BLOCK = 128

PROJ_TILE = (2048, 512)
OUT_TILE = (2048, 512)
MLP_TILE = (1024, 512)
NORM_ROWS = 128
S5_ROWS = 256
S5_SEGS = SUBLANES
S5_SEG = S5_ROWS // S5_SEGS
S5_PITCH = S5_SEG + 8
S5_TILE_GROUPS = LANES // S5_GROUP_SIZE
S5_TILE_STATES = S5_TILE_GROUPS * S5_STATE
HGRN_ROWS = 512


def _cparams(*sem):
    return pltpu.CompilerParams(dimension_semantics=sem, vmem_limit_bytes=VMEM_LIMIT)


def _split_bf16(x):
    hi = x.astype(BF16)
    lo = (x - hi.astype(F32)).astype(BF16)
    return hi, lo


def _rmsnorm_rows(h_ref, g_ref, xn_ref):
    rows = h_ref.shape[0]
    g = g_ref[...]

    def body(i, carry):
        r = pl.multiple_of(i * NORM_ROWS, NORM_ROWS)
        x = h_ref[pl.ds(r, NORM_ROWS), :]
        ms = jnp.mean(x * x, axis=-1, keepdims=True)
        xn_ref[pl.ds(r, NORM_ROWS), :] = (x * lax.rsqrt(ms + EPS) * g).astype(BF16)
        return carry

    lax.fori_loop(0, rows // NORM_ROWS, body, 0)


def _prefetched_rows(h_hbm, h_buf, h_sem, consume):
    i = pl.program_id(0)
    j = pl.program_id(1)
    tm = h_buf.shape[0]

    def h_copy(tile):
        return pltpu.make_async_copy(h_hbm.at[pl.ds(tile * tm, tm), :], h_buf, h_sem)

    @pl.when((i == 0) & (j == 0))
    def _():
        h_copy(0).start()

    @pl.when(j == 0)
    def _():
        h_copy(i).wait()
        consume()

    @pl.when((j == 1) & (i + 1 < pl.num_programs(0)))
    def _():
        h_copy(i + 1).start()


def _norm_matmul_kernel(h_hbm, g_ref, w_ref, o_ref, xn_ref, h_buf, h_sem):
    _prefetched_rows(h_hbm, h_buf, h_sem, lambda: _rmsnorm_rows(h_buf, g_ref, xn_ref))
    o_ref[...] = jnp.dot(xn_ref[...], w_ref[...].astype(BF16), preferred_element_type=F32)


def norm_matmul(h, gain, w_stack, layer, tm, tn):
    m, k = h.shape
    n = w_stack.shape[2]
    tm = min(tm, m)
    assert n // tn >= 2
    return pl.pallas_call(
        _norm_matmul_kernel,
        grid=(m // tm, n // tn),
        in_specs=[
            pl.BlockSpec(memory_space=pl.ANY),
            pl.BlockSpec((1, k), lambda i, j: (0, 0)),
            pl.BlockSpec((None, k, tn), lambda i, j: (layer, 0, j)),
        ],
        out_specs=pl.BlockSpec((tm, tn), lambda i, j: (i, j)),
        out_shape=jax.ShapeDtypeStruct((m, n), F32),
        scratch_shapes=[pltpu.VMEM((tm, k), BF16), pltpu.VMEM((tm, k), F32),
                        pltpu.SemaphoreType.DMA(())],
        compiler_params=_cparams("arbitrary", "arbitrary"),
        name="norm_matmul",
    )(h, gain.reshape(1, k), w_stack)


def _proj_residual_kernel(*refs, n_lhs):
    lhs_refs = refs[:n_lhs]
    w_ref, res_ref, o_ref = refs[n_lhs:]
    acc = res_ref[...]
    k0 = 0
    for lhs_ref in lhs_refs:
        kk = lhs_ref.shape[1]
        acc = acc + jnp.dot(lhs_ref[...], w_ref[k0:k0 + kk, :].astype(BF16),
                            preferred_element_type=F32)
        k0 += kk
    o_ref[...] = acc


def proj_residual(lhs_list, w_stack, layer, res, tm, tn):
    m, n = res.shape
    k = w_stack.shape[1]
    tm = min(tm, m)
    in_specs = [pl.BlockSpec((tm, x.shape[1]), lambda i, j: (i, 0)) for x in lhs_list]
    in_specs += [pl.BlockSpec((None, k, tn), lambda i, j: (layer, 0, j)),
                 pl.BlockSpec((tm, tn), lambda i, j: (i, j))]
    return pl.pallas_call(
        functools.partial(_proj_residual_kernel, n_lhs=len(lhs_list)),
        grid=(m // tm, n // tn),
        in_specs=in_specs,
        out_specs=pl.BlockSpec((tm, tn), lambda i, j: (i, j)),
        out_shape=jax.ShapeDtypeStruct((m, n), F32),
        compiler_params=_cparams("parallel", "arbitrary"),
        name="proj_residual",
    )(*lhs_list, w_stack, res)


def _mlp_kernel(h_hbm, g_ref, wu_ref, wd_ref, o_ref, xn_ref, h_buf, h_sem):
    def first_hidden_tile():
        _rmsnorm_rows(h_buf, g_ref, xn_ref)
        o_ref[...] = h_buf[...]

    _prefetched_rows(h_hbm, h_buf, h_sem, first_hidden_tile)
    up = jnp.dot(xn_ref[...], wu_ref[...].astype(BF16), preferred_element_type=F32)
    act = jnp.square(jnp.maximum(up, 0.0)).astype(BF16)
    o_ref[...] += jnp.dot(act, wd_ref[...].astype(BF16), preferred_element_type=F32)


def mlp_residual(h, gain, w_up_stack, w_down_stack, layer, tm, tf):
    m, d = h.shape
    f = w_up_stack.shape[2]
    tm = min(tm, m)
    assert f // tf >= 2
    return pl.pallas_call(
        _mlp_kernel,
        grid=(m // tm, f // tf),
        in_specs=[
            pl.BlockSpec(memory_space=pl.ANY),
            pl.BlockSpec((1, d), lambda i, j: (0, 0)),
            pl.BlockSpec((None, d, tf), lambda i, j: (layer, 0, j)),
            pl.BlockSpec((None, tf, d), lambda i, j: (layer, j, 0)),
        ],
        out_specs=pl.BlockSpec((tm, d), lambda i, j: (i, 0)),
        out_shape=jax.ShapeDtypeStruct((m, d), F32),
        scratch_shapes=[pltpu.VMEM((tm, d), BF16), pltpu.VMEM((tm, d), F32),
                        pltpu.SemaphoreType.DMA(())],
        compiler_params=_cparams("arbitrary", "arbitrary"),
        name="mlp_residual",
    )(h, gain.reshape(1, d), w_up_stack, w_down_stack)


def _s5_prep_kernel(lre_ref, lim_ref, ldt_ref, bre_ref, bim_ref,
                    ar_ref, ai_ref, asr_ref, asi_ref, bbr_ref, bbi_ref):
    lr = jnp.minimum(lre_ref[...], -S5_MIN_DECAY)
    li = lim_ref[...]
    dt = jnp.exp(ldt_ref[...])
    mag = jnp.exp(lr * dt)
    ar = mag * jnp.cos(li * dt)
    ai = mag * jnp.sin(li * dt)
    den = lr * lr + li * li
    zr = ((ar - 1.0) * lr + ai * li) / den
    zi = (ai * lr - (ar - 1.0) * li) / den
    ar_ref[...] = ar
    ai_ref[...] = ai
    pr, pi = ar, ai
    for _ in range(int(math.log2(S5_SEG))):
        pr, pi = pr * pr - pi * pi, 2.0 * pr * pi
    asr_ref[...] = pr
    asi_ref[...] = pi
    for c in range(S5_GROUP_SIZE):
        br = bre_ref[c]
        bi = bim_ref[c]
        bbr_ref[c] = zr * br - zi * bi
        bbi_ref[c] = zr * bi + zi * br


def s5_prep(lam_re, lam_im, log_dt, b_re, b_im):
    g, p = lam_re.shape
    ldt = jnp.broadcast_to(log_dt[:, None], (g, p))
    bre_t = jnp.transpose(b_re, (2, 0, 1))
    bim_t = jnp.transpose(b_im, (2, 0, 1))
    gp = jax.ShapeDtypeStruct((g, p), F32)
    cgp = jax.ShapeDtypeStruct((S5_GROUP_SIZE, g, p), F32)
    return pl.pallas_call(
        _s5_prep_kernel,
        out_shape=(gp, gp, gp, gp, cgp, cgp),
        name="s5_prep",
    )(lam_re, lam_im, ldt, bre_t, bim_t)


def _s5_kernel(u_ref, bblk_ref, ca_ref, cb_ref, ar_ref, ai_ref, asr_ref, asi_ref,
               d_ref, wglu_ref, bglu_ref, o_ref,
               bu_ref, xp_ref, yp_ref, yn_ref, st_ref):
    n_tiles = u_ref.shape[1] // LANES
    half = S5_TILE_STATES // LANES
    slabs = 2 * half

    @pl.when(pl.program_id(1) == 0)
    def _():
        st_ref[...] = jnp.zeros_like(st_ref)

    u = u_ref[...]
    ub = u.astype(BF16)
    for j in range(n_tiles):
        res = jnp.dot(ub[:, j * LANES:(j + 1) * LANES], bblk_ref[j], preferred_element_type=F32)
        for seg in range(S5_SEGS):
            for k in range(slabs):
                bu_ref[j * slabs + k, seg * S5_PITCH:seg * S5_PITCH + S5_SEG, :] = (
                    res[seg * S5_SEG:(seg + 1) * S5_SEG, k * LANES:(k + 1) * LANES])

    row = lax.broadcasted_iota(jnp.int32, (SUBLANES, LANES), 0)

    def tile_scan(j, carry):
        a_r = [ar_ref[j, :, k * LANES:(k + 1) * LANES] for k in range(half)]
        a_i = [ai_ref[j, :, k * LANES:(k + 1) * LANES] for k in range(half)]

        def load(s, k):
            return bu_ref[j * slabs + k, pl.ds(s, S5_SEGS, stride=S5_PITCH), :]

        def step1(s, x):
            xr, xi = x
            nr, ni = [], []
            for k in range(half):
                br = load(s, k)
                bi = load(s, half + k)
                nr.append(a_r[k] * xr[k] - a_i[k] * xi[k] + br)
                ni.append(a_r[k] * xi[k] + a_i[k] * xr[k] + bi)
            return tuple(nr), tuple(ni)

        zero = tuple(jnp.zeros((SUBLANES, LANES), F32) for _ in range(half))
        fr, fi = lax.fori_loop(0, S5_SEG, step1, (zero, zero), unroll=4)

        init_r, init_i, new_r, new_i = [], [], [], []
        for k in range(half):
            sr = asr_ref[j, :, k * LANES:(k + 1) * LANES]
            si = asi_ref[j, :, k * LANES:(k + 1) * LANES]
            pr = st_ref[j, :, k * LANES:(k + 1) * LANES]
            pi = st_ref[j, :, (half + k) * LANES:(half + k + 1) * LANES]
            ir = jnp.where(row == 0, pr, 0.0)
            ii = jnp.where(row == 0, pi, 0.0)
            for seg in range(1, S5_SEGS):
                er = fr[k] + sr * ir - si * ii
                ei = fi[k] + sr * ii + si * ir
                ir = jnp.where(row == seg, pltpu.roll(er, 1, axis=0), ir)
                ii = jnp.where(row == seg, pltpu.roll(ei, 1, axis=0), ii)
            er = fr[k] + sr * ir - si * ii
            ei = fi[k] + sr * ii + si * ir
            init_r.append(ir)
            init_i.append(ii)
            new_r.append(pltpu.roll(er, 1, axis=0))
            new_i.append(pltpu.roll(ei, 1, axis=0))
        for k in range(half):
            st_ref[j, :, k * LANES:(k + 1) * LANES] = new_r[k]
            st_ref[j, :, (half + k) * LANES:(half + k + 1) * LANES] = new_i[k]

        def step2(s2, x):
            xr, xn = x
            outs_r, outs_n = [], []
            for _ in range(2):
                outs_r.append([])
                outs_n.append([])
            for t in range(2):
                s = s2 * 2 + t
                nr, nn = [], []
                for k in range(half):
                    br = load(s, k)
                    bi = load(s, half + k)
                    nr.append(a_r[k] * xr[k] + a_i[k] * xn[k] + br)
                    nn.append(a_r[k] * xn[k] - a_i[k] * xr[k] - bi)
                xr, xn = tuple(nr), tuple(nn)
                outs_r[t] = nr
                outs_n[t] = nn
            r0 = pl.multiple_of(s2 * (2 * S5_SEGS), 2 * S5_SEGS)
            for k in range(half):
                xp_ref[j, pl.ds(r0, 2 * S5_SEGS), k * LANES:(k + 1) * LANES] = (
                    jnp.concatenate([outs_r[0][k], outs_r[1][k]], axis=0).astype(BF16))
                xp_ref[j, pl.ds(r0, 2 * S5_SEGS), (half + k) * LANES:(half + k + 1) * LANES] = (
                    jnp.concatenate([outs_n[0][k], outs_n[1][k]], axis=0).astype(BF16))
            return xr, xn

        neg_i = tuple(-v for v in init_i)
        lax.fori_loop(0, S5_SEG // 2, step2, (tuple(init_r), neg_i), unroll=2)
        return carry

    lax.fori_loop(0, n_tiles, tile_scan, 0)

    for jj in range(n_tiles // 2):
        yy = (jnp.dot(xp_ref[2 * jj], ca_ref[jj], preferred_element_type=F32)
              + jnp.dot(xp_ref[2 * jj + 1], cb_ref[jj], preferred_element_type=F32))
        yp_ref[2 * jj] = yy[:, :LANES]
        yp_ref[2 * jj + 1] = yy[:, LANES:]

    for j in range(n_tiles):
        for seg in range(S5_SEGS):
            yn_ref[seg * S5_SEG:(seg + 1) * S5_SEG, j * LANES:(j + 1) * LANES] = (
                yp_ref[j, pl.ds(seg, S5_SEG, stride=S5_SEGS), :])

    y = yn_ref[...] + d_ref[...] * u
    y = jax.nn.gelu(y)
    z = jnp.dot(y.astype(BF16), wglu_ref[...], preferred_element_type=F32) + bglu_ref[...]
    o_ref[...] = (y * jax.nn.sigmoid(z)).astype(o_ref.dtype)


def _s5_layouts(ar, ai, asr, asi, bbr, bbi, c_re, c_im):
    g, p = ar.shape
    tg = S5_TILE_GROUPS
    nt = g // tg
    eye = jnp.eye(tg, dtype=F32)

    def lanes(a):
        return jnp.broadcast_to(a.reshape(nt, 1, tg * p), (nt, SUBLANES, tg * p))

    bb = jnp.stack([bbr, bbi]).reshape(2, S5_GROUP_SIZE, nt, tg, p)
    bb = jnp.transpose(bb, (2, 3, 1, 0, 4))
    bblk = bb[:, :, :, :, None, :] * eye[None, :, None, None, :, None]
    bblk = bblk.reshape(nt, tg * S5_GROUP_SIZE, 2 * tg * p).astype(BF16)

    cc = jnp.stack([c_re, c_im]).reshape(2, nt, tg, S5_GROUP_SIZE, p)
    cc = jnp.transpose(cc, (1, 0, 2, 4, 3))
    cblk = cc[:, :, :, :, None, :] * eye[None, None, :, None, :, None]
    cblk = cblk.reshape(nt, 2 * tg * p, tg * S5_GROUP_SIZE)
    zeros = jnp.zeros_like(cblk[0::2])
    ca = jnp.concatenate([cblk[0::2], zeros], axis=-1).astype(BF16)
    cb = jnp.concatenate([zeros, cblk[1::2]], axis=-1).astype(BF16)
    return bblk, ca, cb, lanes(ar), lanes(ai), lanes(asr), lanes(asi)


def s5_mixer(proj, batch, params, d_skip, w_glu, b_glu):
    bblk, ca, cb, ar, ai, asr, asi = params
    m = proj.shape[0]
    width = w_glu.shape[0]
    nt = width // LANES
    steps = m // batch // S5_ROWS
    slabs = 2 * S5_TILE_STATES // LANES
    const3 = lambda b, n: (0, 0, 0)
    const2 = lambda b, n: (0, 0)
    return pl.pallas_call(
        _s5_kernel,
        grid=(batch, steps),
        in_specs=[
            pl.BlockSpec((S5_ROWS, width), lambda b, n: (b * steps + n, 0)),
            pl.BlockSpec(bblk.shape, const3),
            pl.BlockSpec(ca.shape, const3),
            pl.BlockSpec(cb.shape, const3),
            pl.BlockSpec(ar.shape, const3),
            pl.BlockSpec(ai.shape, const3),
            pl.BlockSpec(asr.shape, const3),
            pl.BlockSpec(asi.shape, const3),
            pl.BlockSpec((1, width), const2),
            pl.BlockSpec((width, width), const2),
            pl.BlockSpec((1, width), const2),
        ],
        out_specs=pl.BlockSpec((S5_ROWS, width), lambda b, n: (b * steps + n, 0)),
        out_shape=jax.ShapeDtypeStruct((m, width), BF16),
        scratch_shapes=[
            pltpu.VMEM((nt * slabs, S5_SEGS * S5_PITCH, LANES), F32),
            pltpu.VMEM((nt, S5_ROWS, 2 * S5_TILE_STATES), BF16),
            pltpu.VMEM((nt, S5_ROWS, LANES), F32),
            pltpu.VMEM((S5_ROWS, width), F32),
            pltpu.VMEM((nt, SUBLANES, 2 * S5_TILE_STATES), F32),
        ],
        compiler_params=_cparams("arbitrary", "arbitrary"),
        name="s5_mixer",
    )(proj, bblk, ca, cb, ar, ai, asr, asi, d_skip.reshape(1, width),
      w_glu.astype(BF16), b_glu.reshape(1, width))


def _hgrn_kernel(q_ref, f_ref, i_ref, g_ref, lbp_ref, og_ref, cum_ref, o_ref, st_ref, *, layer):
    n_levels = int(math.log2(HGRN_CHUNK))
    c = HGRN_CHUNK
    hd = HGRN_HEAD_DIM

    @pl.when(pl.program_id(2) == 0)
    def _():
        st_ref[...] = jnp.zeros_like(st_ref)

    lbp = lbp_ref[...]
    e = jnp.exp(lbp - jnp.max(lbp, axis=0, keepdims=True))
    prob = e / jnp.sum(e, axis=0, keepdims=True)
    lb = jnp.sum(prob[:layer + 1], axis=0, keepdims=True) - prob[0:1]
    og = og_ref[...]

    rows = q_ref.shape[0]
    n_chunks = rows // c
    ti = lax.broadcasted_iota(jnp.int32, (c, c), 0)
    si = lax.broadcasted_iota(jnp.int32, (c, c), 1)
    diff_bits = ti ^ si
    tril = cum_ref[...]
    tmod = lax.broadcasted_iota(jnp.int32, (rows, hd), 0) & (c - 1)
    nt_dims = (((1,), (1,)), ((), ()))

    def chunk_rows(x, ch):
        return x[ch * c:(ch + 1) * c]

    forget = lb + (1.0 - lb) * jax.nn.sigmoid(f_ref[...])
    logf = jnp.log2(forget)
    kk = 1.0 - forget
    qq = jax.nn.silu(q_ref[...])
    vb = i_ref[...].astype(BF16)

    hi, lo = _split_bf16(logf)
    b = jnp.concatenate(
        [jnp.dot(tril, chunk_rows(hi, ch), preferred_element_type=F32)
         + jnp.dot(tril, chunk_rows(lo, ch), preferred_element_type=F32) for ch in range(n_chunks)],
        axis=0)

    block_end = b
    scores = [jnp.zeros((c, c), F32) for _ in range(n_chunks)]
    for l in range(n_levels):
        blk = 1 << l
        prev_end = pltpu.roll(block_end, blk, axis=0)
        eq = b - jnp.where(tmod < blk, 0.0, prev_end)
        ek = block_end - b
        ql = (qq * jnp.exp2(eq)).astype(BF16)
        kl = (kk * jnp.exp2(ek)).astype(BF16)
        mask = ((diff_bits >> l) == 1) & (ti > si)
        for ch in range(n_chunks):
            sl = lax.dot_general(chunk_rows(ql, ch), chunk_rows(kl, ch), nt_dims,
                                 preferred_element_type=F32)
            scores[ch] = jnp.where(mask, sl, scores[ch])
        next_end = pltpu.roll(block_end, rows - blk, axis=0)
        block_end = jnp.where((tmod & blk) == 0, next_end, block_end)

    dsum = jnp.sum(qq * kk, axis=-1, keepdims=True)
    qe = (qq * jnp.exp2(b)).astype(BF16)
    kd = (kk * jnp.exp2(block_end - b)).astype(BF16)
    chunk_decay = jnp.exp2(block_end)

    st = st_ref[...]
    outs = []
    for ch in range(n_chunks):
        sc = jnp.where(ti == si, chunk_rows(dsum, ch), scores[ch]).astype(BF16)
        v_c = chunk_rows(vb, ch)
        outs.append(jnp.dot(sc, v_c, preferred_element_type=F32)
                    + lax.dot_general(chunk_rows(qe, ch), st.astype(BF16), nt_dims,
                                      preferred_element_type=F32))
        st = (st * chunk_decay[ch * c:ch * c + 1]
              + lax.dot_general(v_c, chunk_rows(kd, ch), (((0,), (0,)), ((), ())),
                                preferred_element_type=F32))
    st_ref[...] = st

    out = jnp.concatenate(outs, axis=0)
    ms = jnp.mean(out * out, axis=-1, keepdims=True)
    o_ref[...] = (out * lax.rsqrt(ms + EPS) * og * jax.nn.silu(g_ref[...])).astype(o_ref.dtype)


def _hgrn_tril():
    c = HGRN_CHUNK
    return (jnp.arange(c)[None, :] <= jnp.arange(c)[:, None]).astype(BF16)


def hgrn_mixer(proj, batch, col0, lb_param, o_gain, layer):
    m = proj.shape[0]
    n_layers, width = lb_param.shape
    heads = width // HGRN_HEAD_DIM
    steps = m // batch // HGRN_ROWS
    hd = HGRN_HEAD_DIM
    cum = _hgrn_tril()

    def col(block):
        return pl.BlockSpec((HGRN_ROWS, hd),
                            lambda b, h, n: (b * steps + n, (col0 + block * width) // hd + h))

    return pl.pallas_call(
        functools.partial(_hgrn_kernel, layer=layer),
        grid=(batch, heads, steps),
        in_specs=[
            col(0), col(1), col(2), col(3),
            pl.BlockSpec((n_layers, hd), lambda b, h, n: (0, h)),
            pl.BlockSpec((1, hd), lambda b, h, n: (0, 0)),
            pl.BlockSpec(cum.shape, lambda b, h, n: (0, 0)),
        ],
        out_specs=pl.BlockSpec((HGRN_ROWS, hd), lambda b, h, n: (b * steps + n, h)),
        out_shape=jax.ShapeDtypeStruct((m, width), BF16),
        scratch_shapes=[pltpu.VMEM((hd, hd), F32)],
        compiler_params=_cparams("arbitrary", "arbitrary", "arbitrary"),
        name="hgrn_mixer",
    )(proj, proj, proj, proj, lb_param, o_gain.reshape(1, hd), cum)


def _attn_kernel(sink_ref, slope_ref, q_ref, kc_ref, vc_ref, kp_ref, vp_ref,
                 qg_ref, kg_ref, ind_ref, indt_ref, o_ref, bias_ref, *, n_kv):
    blk = ATT_BLOCK
    hd = ATT_HEAD_DIM
    nblk = pl.program_id(1)
    ind = ind_ref[...]
    indt = indt_ref[...]

    def head_rms_scale(x):
        w = x.shape[1]
        hi, lo = _split_bf16(x * x)
        ss = (jnp.dot(hi, ind[:w], preferred_element_type=F32)
              + jnp.dot(lo, ind[:w], preferred_element_type=F32))
        rs = lax.rsqrt(ss * (1.0 / hd) + EPS)
        hi, lo = _split_bf16(rs)
        return (jnp.dot(hi, indt[:, :w], preferred_element_type=F32)
                + jnp.dot(lo, indt[:, :w], preferred_element_type=F32))

    @pl.when(nblk <= 1)
    def _():
        t = lax.broadcasted_iota(jnp.int32, (blk, 2 * blk), 0)
        s = lax.broadcasted_iota(jnp.int32, (blk, 2 * blk), 1)
        dist = t + blk - s
        valid = (dist >= 0) & (dist < WINDOW) & ((s >= blk) | (nblk > 0))
        negdist = -dist.astype(F32)
        for h in range(bias_ref.shape[0]):
            tab = jnp.where(valid, (slope_ref[h] * LOG2E) * negdist, -jnp.inf)
            bias_ref[h] = jnp.where(s == 0, sink_ref[h] * LOG2E, tab)

    q = q_ref[...]
    qn = (q * head_rms_scale(q) * qg_ref[...] * (hd ** -0.5 * LOG2E)).astype(BF16)
    k = jnp.concatenate([kp_ref[...], kc_ref[...]], axis=0)
    kn = k * head_rms_scale(k) * kg_ref[...]
    v = jnp.concatenate([vp_ref[...], vc_ref[...]], axis=0)
    lane = lax.broadcasted_iota(jnp.int32, (2 * blk, LANES), 1)
    key_row = lax.broadcasted_iota(jnp.int32, (2 * blk, LANES), 0)
    low = lax.broadcasted_iota(jnp.int32, (blk, LANES), 1) < hd

    tiles = ATT_GROUP // 2
    for kv in range(n_kv):
        tile = kv // 2
        k_t = kn[:, tile * LANES:(tile + 1) * LANES]
        v_t = v[:, tile * LANES:(tile + 1) * LANES]
        own = ((lane < hd) if kv % 2 == 0 else (lane >= hd)) & (key_row > 0)
        k_own = jnp.where(own, k_t, 0.0)
        v_own = jnp.where(own, v_t, 0.0)
        k_half = [None, None]
        v_half = [None, None]
        k_half[kv % 2] = k_own
        v_half[kv % 2] = v_own
        k_half[1 - kv % 2] = pltpu.roll(k_own, hd, axis=1)
        v_half[1 - kv % 2] = pltpu.roll(v_own, hd, axis=1)
        for par in range(2):
            k_half[par] = k_half[par].astype(BF16)
            data_half = (lane < hd) if par == 0 else (lane >= hd)
            v_half[par] = jnp.where(data_half, v_half[par], 1.0).astype(BF16)
        for mt in range(tiles):
            qt = kv * tiles + mt
            q_t = qn[:, qt * LANES:(qt + 1) * LANES]
            pv = []
            for par in range(2):
                sc = lax.dot_general(q_t, k_half[par], (((1,), (1,)), ((), ())),
                                     preferred_element_type=F32) + bias_ref[2 * qt + par]
                mx = jnp.max(sc, axis=-1, keepdims=True)
                p = jnp.exp2(sc - mx).astype(BF16)
                pv.append(jnp.dot(p, v_half[par], preferred_element_type=F32))
            num = jnp.where(low, pv[0], pv[1])
            den = pltpu.roll(jnp.where(low, pv[1], pv[0]), hd, axis=1)
            o_ref[:, qt * LANES:(qt + 1) * LANES] = (num / den).astype(o_ref.dtype)


def attention(qkv, batch, n_heads, n_kv, q_gain, k_gain, sinks):
    m = qkv.shape[0]
    hd = ATT_HEAD_DIM
    qw = n_heads * hd
    kw = n_kv * hd
    steps = m // batch // ATT_BLOCK
    slopes = jnp.exp2(-8.0 * jnp.arange(1, n_heads + 1, dtype=F32) / n_heads)
    head_of_lane = jnp.arange(qw) // hd
    ind = (head_of_lane[:, None] == jnp.arange(LANES)[None, :]).astype(BF16)
    cur = lambda b, n, *_: b * steps + n
    prev = lambda b, n, *_: b * steps + jnp.maximum(n - 1, 0)
    grid_spec = pltpu.PrefetchScalarGridSpec(
        num_scalar_prefetch=2,
        grid=(batch, steps),
        in_specs=[
            pl.BlockSpec((ATT_BLOCK, qw), lambda b, n, *_: (cur(b, n), 0)),
            pl.BlockSpec((ATT_BLOCK, kw), lambda b, n, *_: (cur(b, n), qw // kw)),
            pl.BlockSpec((ATT_BLOCK, kw), lambda b, n, *_: (cur(b, n), qw // kw + 1)),
            pl.BlockSpec((ATT_BLOCK, kw), lambda b, n, *_: (prev(b, n), qw // kw)),
            pl.BlockSpec((ATT_BLOCK, kw), lambda b, n, *_: (prev(b, n), qw // kw + 1)),
            pl.BlockSpec((1, qw), lambda b, n, *_: (0, 0)),
            pl.BlockSpec((1, kw), lambda b, n, *_: (0, 0)),
            pl.BlockSpec(ind.shape, lambda b, n, *_: (0, 0)),
            pl.BlockSpec(ind.shape[::-1], lambda b, n, *_: (0, 0)),
        ],
        out_specs=pl.BlockSpec((ATT_BLOCK, qw), lambda b, n, *_: (cur(b, n), 0)),
        scratch_shapes=[pltpu.VMEM((n_heads, ATT_BLOCK, 2 * ATT_BLOCK), F32)],
    )
    return pl.pallas_call(
        functools.partial(_attn_kernel, n_kv=n_kv),
        grid_spec=grid_spec,
        out_shape=jax.ShapeDtypeStruct((m, qw), BF16),
        compiler_params=_cparams("arbitrary", "arbitrary"),
        name="swa_attention",
    )(sinks, slopes, qkv, qkv, qkv, qkv, qkv,
      jnp.tile(q_gain, n_heads).reshape(1, qw), jnp.tile(k_gain, n_kv).reshape(1, kw),
      ind, ind.T)


def kernel(x, even_norm, even_w_in, s5_lambda_re, s5_lambda_im, s5_log_dt, s5_b_re, s5_b_im,
           s5_c_re, s5_c_im, s5_d, s5_w_glu, s5_b_glu, hgrn_lower_bound, hgrn_o_norm, even_w_out,
           odd_norm, odd_w_qkv, q_norm, k_norm, att_sinks, odd_w_out, mlp_norm, mlp_w_up, mlp_w_down):
    batch, seqlen, d_model = x.shape
    depth = mlp_norm.shape[0]
    s5_width = s5_w_glu.shape[1]
    n_heads = att_sinks.shape[1]
    n_kv = n_heads // ATT_GROUP
    assert seqlen % max(S5_ROWS, HGRN_ROWS, ATT_BLOCK) == 0
    h = x.reshape(batch * seqlen, d_model)
    for layer in range(depth):
        j = layer // 2
        if layer % 2 == 0:
            proj = norm_matmul(h, even_norm[j], even_w_in, j, *PROJ_TILE)
            prep = s5_prep(s5_lambda_re[j], s5_lambda_im[j], s5_log_dt[j], s5_b_re[j], s5_b_im[j])
            params = _s5_layouts(*prep, s5_c_re[j], s5_c_im[j])
            y_a = s5_mixer(proj, batch, params, s5_d[j], s5_w_glu[j], s5_b_glu[j])
            y_b = hgrn_mixer(proj, batch, s5_width, hgrn_lower_bound, hgrn_o_norm[j], j)
            h = proj_residual([y_a, y_b], even_w_out, j, h, *OUT_TILE)
        else:
            qkv = norm_matmul(h, odd_norm[j], odd_w_qkv, j, *PROJ_TILE)
            o = attention(qkv, batch, n_heads, n_kv, q_norm[j], k_norm[j], att_sinks[j])
            h = proj_residual([o], odd_w_out, j, h, *OUT_TILE)
        h = mlp_residual(h, mlp_norm[layer], mlp_w_up, mlp_w_down, layer, *MLP_TILE)
    return h.reshape(batch, seqlen, d_model)
```

```python
import functools
import math

import jax
import jax.numpy as jnp
from jax import lax
from jax.experimental import pallas as pl
from jax.experimental.pallas import tpu as pltpu

F32 = jnp.float32
BF16 = jnp.bfloat16
EPS = 1e-6
LOG2E = math.log2(math.e)

LANES = 128
SUBLANES = 8
VMEM_LIMIT = 56 * 1024 * 1024

S5_GROUP_SIZE = 16
S5_STATE = 64
S5_MIN_DECAY = 1e-4
HGRN_HEAD_DIM = 128
HGRN_CHUNK = 64
ATT_HEAD_DIM = 64
ATT_GROUP = 8
WINDOW = 128
ATT_BLOCK = 128

PROJ_TILE = (2048, 512)
OUT_TILE = (2048, 512)
MLP_TILE = (1024, 512)
NORM_ROWS = 128
S5_ROWS = 512
S5_SEGS = SUBLANES
S5_SEG = S5_ROWS // S5_SEGS
S5_PITCH = S5_SEG + 8
S5_TILE_GROUPS = LANES // S5_GROUP_SIZE
S5_TILE_STATES = S5_TILE_GROUPS * S5_STATE
HGRN_ROWS = 512
HGRN_STEP_HEADS = 2


def _cparams(*sem):
    return pltpu.CompilerParams(dimension_semantics=sem, vmem_limit_bytes=VMEM_LIMIT)


def _split_bf16(x):
    hi = x.astype(BF16)
    lo = (x - hi.astype(F32)).astype(BF16)
    return hi, lo


def _rmsnorm_rows(h_ref, g_ref, xn_ref):
    rows = h_ref.shape[0]
    g = g_ref[...]

    def body(i, carry):
        r = pl.multiple_of(i * NORM_ROWS, NORM_ROWS)
        x = h_ref[pl.ds(r, NORM_ROWS), :]
        ms = jnp.mean(x * x, axis=-1, keepdims=True)
        xn_ref[pl.ds(r, NORM_ROWS), :] = (x * lax.rsqrt(ms + EPS) * g).astype(BF16)
        return carry

    lax.fori_loop(0, rows // NORM_ROWS, body, 0)


def _prefetched_rows(h_hbm, h_buf, h_sem, consume):
    i = pl.program_id(0)
    j = pl.program_id(1)
    tm = h_buf.shape[0]

    def h_copy(tile):
        return pltpu.make_async_copy(h_hbm.at[pl.ds(tile * tm, tm), :], h_buf, h_sem)

    @pl.when((i == 0) & (j == 0))
    def _():
        h_copy(0).start()

    @pl.when(j == 0)
    def _():
        h_copy(i).wait()
        consume()

    @pl.when((j == 1) & (i + 1 < pl.num_programs(0)))
    def _():
        h_copy(i + 1).start()


def _norm_matmul_kernel(h_hbm, g_ref, w_ref, o_ref, xn_ref, h_buf, h_sem):
    _prefetched_rows(h_hbm, h_buf, h_sem, lambda: _rmsnorm_rows(h_buf, g_ref, xn_ref))
    o_ref[...] = jnp.dot(xn_ref[...], w_ref[...].astype(BF16), preferred_element_type=F32)


def norm_matmul(h, gain, w_stack, layer, tm, tn):
    m, k = h.shape
    n = w_stack.shape[2]
    tm = min(tm, m)
    assert n // tn >= 2
    return pl.pallas_call(
        _norm_matmul_kernel,
        grid=(m // tm, n // tn),
        in_specs=[
            pl.BlockSpec(memory_space=pl.ANY),
            pl.BlockSpec((1, k), lambda i, j: (0, 0)),
            pl.BlockSpec((None, k, tn), lambda i, j: (layer, 0, j)),
        ],
        out_specs=pl.BlockSpec((tm, tn), lambda i, j: (i, j)),
        out_shape=jax.ShapeDtypeStruct((m, n), F32),
        scratch_shapes=[pltpu.VMEM((tm, k), BF16), pltpu.VMEM((tm, k), F32),
                        pltpu.SemaphoreType.DMA(())],
        compiler_params=_cparams("arbitrary", "arbitrary"),
        name="norm_matmul",
    )(h, gain.reshape(1, k), w_stack)


def _proj_residual_kernel(*refs, n_lhs):
    lhs_refs = refs[:n_lhs]
    w_ref, res_ref, o_ref = refs[n_lhs:]
    acc = res_ref[...]
    k0 = 0
    for lhs_ref in lhs_refs:
        kk = lhs_ref.shape[1]
        acc = acc + jnp.dot(lhs_ref[...], w_ref[k0:k0 + kk, :].astype(BF16),
                            preferred_element_type=F32)
        k0 += kk
    o_ref[...] = acc


def proj_residual(lhs_list, w_stack, layer, res, tm, tn):
    m, n = res.shape
    k = w_stack.shape[1]
    tm = min(tm, m)
    in_specs = [pl.BlockSpec((tm, x.shape[1]), lambda i, j: (i, 0)) for x in lhs_list]
    in_specs += [pl.BlockSpec((None, k, tn), lambda i, j: (layer, 0, j)),
                 pl.BlockSpec((tm, tn), lambda i, j: (i, j))]
    return pl.pallas_call(
        functools.partial(_proj_residual_kernel, n_lhs=len(lhs_list)),
        grid=(m // tm, n // tn),
        in_specs=in_specs,
        out_specs=pl.BlockSpec((tm, tn), lambda i, j: (i, j)),
        out_shape=jax.ShapeDtypeStruct((m, n), F32),
        compiler_params=_cparams("parallel", "arbitrary"),
        name="proj_residual",
    )(*lhs_list, w_stack, res)


def _mlp_kernel(h_hbm, g_ref, wu_ref, wd_ref, o_ref, xn_ref, h_buf, h_sem):
    def first_hidden_tile():
        _rmsnorm_rows(h_buf, g_ref, xn_ref)
        o_ref[...] = h_buf[...]

    _prefetched_rows(h_hbm, h_buf, h_sem, first_hidden_tile)
    up = jnp.dot(xn_ref[...], wu_ref[...].astype(BF16), preferred_element_type=F32)
    act = jnp.square(jnp.maximum(up, 0.0)).astype(BF16)
    o_ref[...] += jnp.dot(act, wd_ref[...].astype(BF16), preferred_element_type=F32)


def mlp_residual(h, gain, w_up_stack, w_down_stack, layer, tm, tf):
    m, d = h.shape
    f = w_up_stack.shape[2]
    tm = min(tm, m)
    assert f // tf >= 2
    return pl.pallas_call(
        _mlp_kernel,
        grid=(m // tm, f // tf),
        in_specs=[
            pl.BlockSpec(memory_space=pl.ANY),
            pl.BlockSpec((1, d), lambda i, j: (0, 0)),
            pl.BlockSpec((None, d, tf), lambda i, j: (layer, 0, j)),
            pl.BlockSpec((None, tf, d), lambda i, j: (layer, j, 0)),
        ],
        out_specs=pl.BlockSpec((tm, d), lambda i, j: (i, 0)),
        out_shape=jax.ShapeDtypeStruct((m, d), F32),
        scratch_shapes=[pltpu.VMEM((tm, d), BF16), pltpu.VMEM((tm, d), F32),
                        pltpu.SemaphoreType.DMA(())],
        compiler_params=_cparams("arbitrary", "arbitrary"),
        name="mlp_residual",
    )(h, gain.reshape(1, d), w_up_stack, w_down_stack)


def _s5_prep_kernel(lre_ref, lim_ref, ldt_ref, bre_ref, bim_ref,
                    ar_ref, ai_ref, asr_ref, asi_ref, bbr_ref, bbi_ref):
    lr = jnp.minimum(lre_ref[...], -S5_MIN_DECAY)
    li = lim_ref[...]
    dt = jnp.exp(ldt_ref[...])
    mag = jnp.exp(lr * dt)
    ar = mag * jnp.cos(li * dt)
    ai = mag * jnp.sin(li * dt)
    den = lr * lr + li * li
    zr = ((ar - 1.0) * lr + ai * li) / den
    zi = (ai * lr - (ar - 1.0) * li) / den
    ar_ref[...] = ar
    ai_ref[...] = ai
    pr, pi = ar, ai
    for _ in range(int(math.log2(S5_SEG))):
        pr, pi = pr * pr - pi * pi, 2.0 * pr * pi
    asr_ref[...] = pr
    asi_ref[...] = pi
    for c in range(S5_GROUP_SIZE):
        br = bre_ref[c]
        bi = bim_ref[c]
        bbr_ref[c] = zr * br - zi * bi
        bbi_ref[c] = zr * bi + zi * br


def s5_prep(lam_re, lam_im, log_dt, b_re, b_im):
    g, p = lam_re.shape
    ldt = jnp.broadcast_to(log_dt[:, None], (g, p))
    bre_t = jnp.transpose(b_re, (2, 0, 1))
    bim_t = jnp.transpose(b_im, (2, 0, 1))
    gp = jax.ShapeDtypeStruct((g, p), F32)
    cgp = jax.ShapeDtypeStruct((S5_GROUP_SIZE, g, p), F32)
    return pl.pallas_call(
        _s5_prep_kernel,
        out_shape=(gp, gp, gp, gp, cgp, cgp),
        name="s5_prep",
    )(lam_re, lam_im, ldt, bre_t, bim_t)


def _s5_kernel(u_ref, bblk_ref, ca_ref, cb_ref, ar_ref, ai_ref, asr_ref, asi_ref,
               d_ref, wglu_ref, bglu_ref, o_ref,
               bu_ref, xp_ref, yp_ref, yn_ref, st_ref):
    n_tiles = u_ref.shape[1] // LANES
    half = S5_TILE_STATES // LANES
    slabs = 2 * half

    @pl.when(pl.program_id(1) == 0)
    def _():
        st_ref[...] = jnp.zeros_like(st_ref)

    u = u_ref[...]
    ub = u.astype(BF16)
    for j in range(n_tiles):
        res = jnp.dot(ub[:, j * LANES:(j + 1) * LANES], bblk_ref[j], preferred_element_type=F32)
        for seg in range(S5_SEGS):
            for k in range(slabs):
                bu_ref[j * slabs + k, seg * S5_PITCH:seg * S5_PITCH + S5_SEG, :] = (
                    res[seg * S5_SEG:(seg + 1) * S5_SEG, k * LANES:(k + 1) * LANES])

    row = lax.broadcasted_iota(jnp.int32, (SUBLANES, LANES), 0)

    def tile_scan(j, carry):
        a_r = [ar_ref[j, :, k * LANES:(k + 1) * LANES] for k in range(half)]
        a_i = [ai_ref[j, :, k * LANES:(k + 1) * LANES] for k in range(half)]

        def load(s, k):
            return bu_ref[j * slabs + k, pl.ds(s, S5_SEGS, stride=S5_PITCH), :]

        def step1(s, x):
            xr, xi = x
            nr, ni = [], []
            for k in range(half):
                br = load(s, k)
                bi = load(s, half + k)
                nr.append(a_r[k] * xr[k] - a_i[k] * xi[k] + br)
                ni.append(a_r[k] * xi[k] + a_i[k] * xr[k] + bi)
            return tuple(nr), tuple(ni)

        zero = tuple(jnp.zeros((SUBLANES, LANES), F32) for _ in range(half))
        fr, fi = lax.fori_loop(0, S5_SEG, step1, (zero, zero), unroll=4)

        init_r, init_i, new_r, new_i = [], [], [], []
        for k in range(half):
            sr = asr_ref[j, :, k * LANES:(k + 1) * LANES]
            si = asi_ref[j, :, k * LANES:(k + 1) * LANES]
            pr = st_ref[j, :, k * LANES:(k + 1) * LANES]
            pi = st_ref[j, :, (half + k) * LANES:(half + k + 1) * LANES]
            ir = jnp.where(row == 0, pr, 0.0)
            ii = jnp.where(row == 0, pi, 0.0)
            for seg in range(1, S5_SEGS):
                er = fr[k] + sr * ir - si * ii
                ei = fi[k] + sr * ii + si * ir
                ir = jnp.where(row == seg, pltpu.roll(er, 1, axis=0), ir)
                ii = jnp.where(row == seg, pltpu.roll(ei, 1, axis=0), ii)
            er = fr[k] + sr * ir - si * ii
            ei = fi[k] + sr * ii + si * ir
            init_r.append(ir)
            init_i.append(ii)
            new_r.append(pltpu.roll(er, 1, axis=0))
            new_i.append(pltpu.roll(ei, 1, axis=0))
        for k in range(half):
            st_ref[j, :, k * LANES:(k + 1) * LANES] = new_r[k]
            st_ref[j, :, (half + k) * LANES:(half + k + 1) * LANES] = new_i[k]

        def step2(s2, x):
            xr, xn = x
            outs_r, outs_n = [], []
            for _ in range(2):
                outs_r.append([])
                outs_n.append([])
            for t in range(2):
                s = s2 * 2 + t
                nr, nn = [], []
                for k in range(half):
                    br = load(s, k)
                    bi = load(s, half + k)
                    nr.append(a_r[k] * xr[k] + a_i[k] * xn[k] + br)
                    nn.append(a_r[k] * xn[k] - a_i[k] * xr[k] - bi)
                xr, xn = tuple(nr), tuple(nn)
                outs_r[t] = nr
                outs_n[t] = nn
            r0 = pl.multiple_of(s2 * (2 * S5_SEGS), 2 * S5_SEGS)
            for k in range(half):
                xp_ref[j, pl.ds(r0, 2 * S5_SEGS), k * LANES:(k + 1) * LANES] = (
                    jnp.concatenate([outs_r[0][k], outs_r[1][k]], axis=0).astype(BF16))
                xp_ref[j, pl.ds(r0, 2 * S5_SEGS), (half + k) * LANES:(half + k + 1) * LANES] = (
                    jnp.concatenate([outs_n[0][k], outs_n[1][k]], axis=0).astype(BF16))
            return xr, xn

        neg_i = tuple(-v for v in init_i)
        lax.fori_loop(0, S5_SEG // 2, step2, (tuple(init_r), neg_i), unroll=2)
        return carry

    lax.fori_loop(0, n_tiles, tile_scan, 0)

    for jj in range(n_tiles // 2):
        yy = (jnp.dot(xp_ref[2 * jj], ca_ref[jj], preferred_element_type=F32)
              + jnp.dot(xp_ref[2 * jj + 1], cb_ref[jj], preferred_element_type=F32))
        yp_ref[2 * jj] = yy[:, :LANES]
        yp_ref[2 * jj + 1] = yy[:, LANES:]

    for j in range(n_tiles):
        for seg in range(S5_SEGS):
            yn_ref[seg * S5_SEG:(seg + 1) * S5_SEG, j * LANES:(j + 1) * LANES] = (
                yp_ref[j, pl.ds(seg, S5_SEG, stride=S5_SEGS), :])

    y = yn_ref[...] + d_ref[...] * u
    y = jax.nn.gelu(y)
    z = jnp.dot(y.astype(BF16), wglu_ref[...], preferred_element_type=F32) + bglu_ref[...]
    o_ref[...] = (y * jax.nn.sigmoid(z)).astype(o_ref.dtype)


def _s5_layouts(ar, ai, asr, asi, bbr, bbi, c_re, c_im):
    g, p = ar.shape
    tg = S5_TILE_GROUPS
    nt = g // tg
    eye = jnp.eye(tg, dtype=F32)

    def lanes(a):
        return jnp.broadcast_to(a.reshape(nt, 1, tg * p), (nt, SUBLANES, tg * p))

    bb = jnp.stack([bbr, bbi]).reshape(2, S5_GROUP_SIZE, nt, tg, p)
    bb = jnp.transpose(bb, (2, 3, 1, 0, 4))
    bblk = bb[:, :, :, :, None, :] * eye[None, :, None, None, :, None]
    bblk = bblk.reshape(nt, tg * S5_GROUP_SIZE, 2 * tg * p).astype(BF16)

    cc = jnp.stack([c_re, c_im]).reshape(2, nt, tg, S5_GROUP_SIZE, p)
    cc = jnp.transpose(cc, (1, 0, 2, 4, 3))
    cblk = cc[:, :, :, :, None, :] * eye[None, None, :, None, :, None]
    cblk = cblk.reshape(nt, 2 * tg * p, tg * S5_GROUP_SIZE)
    zeros = jnp.zeros_like(cblk[0::2])
    ca = jnp.concatenate([cblk[0::2], zeros], axis=-1).astype(BF16)
    cb = jnp.concatenate([zeros, cblk[1::2]], axis=-1).astype(BF16)
    return bblk, ca, cb, lanes(ar), lanes(ai), lanes(asr), lanes(asi)


def s5_mixer(proj, batch, params, d_skip, w_glu, b_glu):
    bblk, ca, cb, ar, ai, asr, asi = params
    m = proj.shape[0]
    width = w_glu.shape[0]
    nt = width // LANES
    steps = m // batch // S5_ROWS
    slabs = 2 * S5_TILE_STATES // LANES
    const3 = lambda b, n: (0, 0, 0)
    const2 = lambda b, n: (0, 0)
    once = pl.Buffered(1)
    return pl.pallas_call(
        _s5_kernel,
        grid=(batch, steps),
        in_specs=[
            pl.BlockSpec((S5_ROWS, width), lambda b, n: (b * steps + n, 0)),
            pl.BlockSpec(bblk.shape, const3, pipeline_mode=once),
            pl.BlockSpec(ca.shape, const3, pipeline_mode=once),
            pl.BlockSpec(cb.shape, const3, pipeline_mode=once),
            pl.BlockSpec(ar.shape, const3),
            pl.BlockSpec(ai.shape, const3),
            pl.BlockSpec(asr.shape, const3),
            pl.BlockSpec(asi.shape, const3),
            pl.BlockSpec((1, width), const2),
            pl.BlockSpec((width, width), const2, pipeline_mode=once),
            pl.BlockSpec((1, width), const2),
        ],
        out_specs=pl.BlockSpec((S5_ROWS, width), lambda b, n: (b * steps + n, 0)),
        out_shape=jax.ShapeDtypeStruct((m, width), BF16),
        scratch_shapes=[
            pltpu.VMEM((nt * slabs, S5_SEGS * S5_PITCH, LANES), F32),
            pltpu.VMEM((nt, S5_ROWS, 2 * S5_TILE_STATES), BF16),
            pltpu.VMEM((nt, S5_ROWS, LANES), F32),
            pltpu.VMEM((S5_ROWS, width), F32),
            pltpu.VMEM((nt, SUBLANES, 2 * S5_TILE_STATES), F32),
        ],
        compiler_params=_cparams("arbitrary", "arbitrary"),
        name="s5_mixer",
    )(proj, bblk, ca, cb, ar, ai, asr, asi, d_skip.reshape(1, width),
      w_glu.astype(BF16), b_glu.reshape(1, width))


def _hgrn_kernel(q_ref, f_ref, i_ref, g_ref, lbp_ref, og_ref, cum_ref, o_ref, st_ref, *, layer):
    @pl.when(pl.program_id(2) == 0)
    def _():
        st_ref[...] = jnp.zeros_like(st_ref)

    for hh in range(st_ref.shape[0]):
        lanes = pl.ds(hh * HGRN_HEAD_DIM, HGRN_HEAD_DIM)
        _hgrn_head(q_ref.at[:, lanes], f_ref.at[:, lanes], i_ref.at[:, lanes], g_ref.at[:, lanes],
                   lbp_ref.at[:, lanes], og_ref, cum_ref, o_ref.at[:, lanes], st_ref.at[hh], layer)


def _hgrn_head(q_ref, f_ref, i_ref, g_ref, lbp_ref, og_ref, cum_ref, o_ref, st_ref, layer):
    n_levels = int(math.log2(HGRN_CHUNK))
    c = HGRN_CHUNK
    hd = HGRN_HEAD_DIM

    lbp = lbp_ref[...]
    e = jnp.exp(lbp - jnp.max(lbp, axis=0, keepdims=True))
    prob = e / jnp.sum(e, axis=0, keepdims=True)
    lb = jnp.sum(prob[:layer + 1], axis=0, keepdims=True) - prob[0:1]
    og = og_ref[...]

    rows = q_ref.shape[0]
    n_chunks = rows // c
    ti = lax.broadcasted_iota(jnp.int32, (c, c), 0)
    si = lax.broadcasted_iota(jnp.int32, (c, c), 1)
    diff_bits = ti ^ si
    tril = cum_ref[...]
    tmod = lax.broadcasted_iota(jnp.int32, (rows, hd), 0) & (c - 1)
    nt_dims = (((1,), (1,)), ((), ()))

    def chunk_rows(x, ch):
        return x[ch * c:(ch + 1) * c]

    forget = lb + (1.0 - lb) * jax.nn.sigmoid(f_ref[...])
    logf = jnp.log2(forget)
    kk = 1.0 - forget
    qq = jax.nn.silu(q_ref[...])
    vb = i_ref[...].astype(BF16)

    hi, lo = _split_bf16(logf)
    b = jnp.concatenate(
        [jnp.dot(tril, chunk_rows(hi, ch), preferred_element_type=F32)
         + jnp.dot(tril, chunk_rows(lo, ch), preferred_element_type=F32) for ch in range(n_chunks)],
        axis=0)

    block_end = b
    scores = [jnp.zeros((c, c), F32) for _ in range(n_chunks)]
    for l in range(n_levels):
        blk = 1 << l
        prev_end = pltpu.roll(block_end, blk, axis=0)
        eq = b - jnp.where(tmod < blk, 0.0, prev_end)
        ek = block_end - b
        ql = (qq * jnp.exp2(eq)).astype(BF16)
        kl = (kk * jnp.exp2(ek)).astype(BF16)
        mask = ((diff_bits >> l) == 1) & (ti > si)
        for ch in range(n_chunks):
            sl = lax.dot_general(chunk_rows(ql, ch), chunk_rows(kl, ch), nt_dims,
                                 preferred_element_type=F32)
            scores[ch] = jnp.where(mask, sl, scores[ch])
        next_end = pltpu.roll(block_end, rows - blk, axis=0)
        block_end = jnp.where((tmod & blk) == 0, next_end, block_end)

    dsum = jnp.sum(qq * kk, axis=-1, keepdims=True)
    qe = (qq * jnp.exp2(b)).astype(BF16)
    kd = (kk * jnp.exp2(block_end - b)).astype(BF16)
    chunk_decay = jnp.exp2(block_end)

    st = st_ref[...]
    outs = []
    for ch in range(n_chunks):
        sc = jnp.where(ti == si, chunk_rows(dsum, ch), scores[ch]).astype(BF16)
        v_c = chunk_rows(vb, ch)
        outs.append(jnp.dot(sc, v_c, preferred_element_type=F32)
                    + lax.dot_general(chunk_rows(qe, ch), st.astype(BF16), nt_dims,
                                      preferred_element_type=F32))
        st = (st * chunk_decay[ch * c:ch * c + 1]
              + lax.dot_general(v_c, chunk_rows(kd, ch), (((0,), (0,)), ((), ())),
                                preferred_element_type=F32))
    st_ref[...] = st

    out = jnp.concatenate(outs, axis=0)
    ms = jnp.mean(out * out, axis=-1, keepdims=True)
    o_ref[...] = (out * lax.rsqrt(ms + EPS) * og * jax.nn.silu(g_ref[...])).astype(o_ref.dtype)


def _hgrn_tril():
    c = HGRN_CHUNK
    return (jnp.arange(c)[None, :] <= jnp.arange(c)[:, None]).astype(BF16)


def hgrn_mixer(proj, batch, col0, lb_param, o_gain, layer):
    m = proj.shape[0]
    n_layers, width = lb_param.shape
    heads = width // HGRN_HEAD_DIM
    steps = m // batch // HGRN_ROWS
    hd = HGRN_HEAD_DIM
    cum = _hgrn_tril()

    hw = HGRN_STEP_HEADS * hd

    def col(block):
        return pl.BlockSpec((HGRN_ROWS, hw),
                            lambda b, h, n: (b * steps + n, (col0 + block * width) // hw + h))

    return pl.pallas_call(
        functools.partial(_hgrn_kernel, layer=layer),
        grid=(batch, heads // HGRN_STEP_HEADS, steps),
        in_specs=[
            col(0), col(1), col(2), col(3),
            pl.BlockSpec((n_layers, hw), lambda b, h, n: (0, h)),
            pl.BlockSpec((1, hd), lambda b, h, n: (0, 0)),
            pl.BlockSpec(cum.shape, lambda b, h, n: (0, 0)),
        ],
        out_specs=pl.BlockSpec((HGRN_ROWS, hw), lambda b, h, n: (b * steps + n, h)),
        out_shape=jax.ShapeDtypeStruct((m, width), BF16),
        scratch_shapes=[pltpu.VMEM((HGRN_STEP_HEADS, hd, hd), F32)],
        compiler_params=_cparams("arbitrary", "arbitrary", "arbitrary"),
        name="hgrn_mixer",
    )(proj, proj, proj, proj, lb_param, o_gain.reshape(1, hd), cum)


def _attn_kernel(sink_ref, slope_ref, q_ref, kc_ref, vc_ref, kp_ref, vp_ref,
                 qg_ref, kg_ref, ind_ref, indt_ref, o_ref, bias_ref, *, n_kv):
    blk = ATT_BLOCK
    hd = ATT_HEAD_DIM
    nblk = pl.program_id(1)
    ind = ind_ref[...]
    indt = indt_ref[...]

    def head_rms_scale(x):
        w = x.shape[1]
        hi, lo = _split_bf16(x * x)
        ss = (jnp.dot(hi, ind[:w], preferred_element_type=F32)
              + jnp.dot(lo, ind[:w], preferred_element_type=F32))
        rs = lax.rsqrt(ss * (1.0 / hd) + EPS)
        hi, lo = _split_bf16(rs)
        return (jnp.dot(hi, indt[:, :w], preferred_element_type=F32)
                + jnp.dot(lo, indt[:, :w], preferred_element_type=F32))

    @pl.when(nblk <= 1)
    def _():
        t = lax.broadcasted_iota(jnp.int32, (blk, 2 * blk), 0)
        s = lax.broadcasted_iota(jnp.int32, (blk, 2 * blk), 1)
        dist = t + blk - s
        valid = (dist >= 0) & (dist < WINDOW) & ((s >= blk) | (nblk > 0))
        negdist = -dist.astype(F32)
        for h in range(bias_ref.shape[0]):
            tab = jnp.where(valid, (slope_ref[h] * LOG2E) * negdist, -jnp.inf)
            bias_ref[h] = jnp.where(s == 0, sink_ref[h] * LOG2E, tab)

    q = q_ref[...]
    qn = (q * head_rms_scale(q) * qg_ref[...] * (hd ** -0.5 * LOG2E)).astype(BF16)
    k = jnp.concatenate([kp_ref[...], kc_ref[...]], axis=0)
    kn = k * head_rms_scale(k) * kg_ref[...]
    v = jnp.concatenate([vp_ref[...], vc_ref[...]], axis=0)
    lane = lax.broadcasted_iota(jnp.int32, (2 * blk, LANES), 1)
    key_row = lax.broadcasted_iota(jnp.int32, (2 * blk, LANES), 0)
    low = lax.broadcasted_iota(jnp.int32, (blk, LANES), 1) < hd

    tiles = ATT_GROUP // 2
    for kv in range(n_kv):
        tile = kv // 2
        k_t = kn[:, tile * LANES:(tile + 1) * LANES]
        v_t = v[:, tile * LANES:(tile + 1) * LANES]
        own = ((lane < hd) if kv % 2 == 0 else (lane >= hd)) & (key_row > 0)
        k_own = jnp.where(own, k_t, 0.0)
        v_own = jnp.where(own, v_t, 0.0)
        k_half = [None, None]
        v_half = [None, None]
        k_half[kv % 2] = k_own
        v_half[kv % 2] = v_own
        k_half[1 - kv % 2] = pltpu.roll(k_own, hd, axis=1)
        v_half[1 - kv % 2] = pltpu.roll(v_own, hd, axis=1)
        for par in range(2):
            k_half[par] = k_half[par].astype(BF16)
            data_half = (lane < hd) if par == 0 else (lane >= hd)
            v_half[par] = jnp.where(data_half, v_half[par], 1.0).astype(BF16)
        for mt in range(tiles):
            qt = kv * tiles + mt
            q_t = qn[:, qt * LANES:(qt + 1) * LANES]
            pv = []
            for par in range(2):
                sc = lax.dot_general(q_t, k_half[par], (((1,), (1,)), ((), ())),
                                     preferred_element_type=F32) + bias_ref[2 * qt + par]
                mx = jnp.max(sc, axis=-1, keepdims=True)
                p = jnp.exp2(sc - mx).astype(BF16)
                pv.append(jnp.dot(p, v_half[par], preferred_element_type=F32))
            num = jnp.where(low, pv[0], pv[1])
            den = pltpu.roll(jnp.where(low, pv[1], pv[0]), hd, axis=1)
            o_ref[:, qt * LANES:(qt + 1) * LANES] = (num / den).astype(o_ref.dtype)


def attention(qkv, batch, n_heads, n_kv, q_gain, k_gain, sinks):
    m = qkv.shape[0]
    hd = ATT_HEAD_DIM
    qw = n_heads * hd
    kw = n_kv * hd
    steps = m // batch // ATT_BLOCK
    slopes = jnp.exp2(-8.0 * jnp.arange(1, n_heads + 1, dtype=F32) / n_heads)
    head_of_lane = jnp.arange(qw) // hd
    ind = (head_of_lane[:, None] == jnp.arange(LANES)[None, :]).astype(BF16)
    cur = lambda b, n, *_: b * steps + n
    prev = lambda b, n, *_: b * steps + jnp.maximum(n - 1, 0)
    grid_spec = pltpu.PrefetchScalarGridSpec(
        num_scalar_prefetch=2,
        grid=(batch, steps),
        in_specs=[
            pl.BlockSpec((ATT_BLOCK, qw), lambda b, n, *_: (cur(b, n), 0)),
            pl.BlockSpec((ATT_BLOCK, kw), lambda b, n, *_: (cur(b, n), qw // kw)),
            pl.BlockSpec((ATT_BLOCK, kw), lambda b, n, *_: (cur(b, n), qw // kw + 1)),
            pl.BlockSpec((ATT_BLOCK, kw), lambda b, n, *_: (prev(b, n), qw // kw)),
            pl.BlockSpec((ATT_BLOCK, kw), lambda b, n, *_: (prev(b, n), qw // kw + 1)),
            pl.BlockSpec((1, qw), lambda b, n, *_: (0, 0)),
            pl.BlockSpec((1, kw), lambda b, n, *_: (0, 0)),
            pl.BlockSpec(ind.shape, lambda b, n, *_: (0, 0)),
            pl.BlockSpec(ind.shape[::-1], lambda b, n, *_: (0, 0)),
        ],
        out_specs=pl.BlockSpec((ATT_BLOCK, qw), lambda b, n, *_: (cur(b, n), 0)),
        scratch_shapes=[pltpu.VMEM((n_heads, ATT_BLOCK, 2 * ATT_BLOCK), F32)],
    )
    return pl.pallas_call(
        functools.partial(_attn_kernel, n_kv=n_kv),
        grid_spec=grid_spec,
        out_shape=jax.ShapeDtypeStruct((m, qw), BF16),
        compiler_params=_cparams("arbitrary", "arbitrary"),
        name="swa_attention",
    )(sinks, slopes, qkv, qkv, qkv, qkv, qkv,
      jnp.tile(q_gain, n_heads).reshape(1, qw), jnp.tile(k_gain, n_kv).reshape(1, kw),
      ind, ind.T)


def kernel(x, even_norm, even_w_in, s5_lambda_re, s5_lambda_im, s5_log_dt, s5_b_re, s5_b_im,
           s5_c_re, s5_c_im, s5_d, s5_w_glu, s5_b_glu, hgrn_lower_bound, hgrn_o_norm, even_w_out,
           odd_norm, odd_w_qkv, q_norm, k_norm, att_sinks, odd_w_out, mlp_norm, mlp_w_up, mlp_w_down):
    batch, seqlen, d_model = x.shape
    depth = mlp_norm.shape[0]
    s5_width = s5_w_glu.shape[1]
    n_heads = att_sinks.shape[1]
    n_kv = n_heads // ATT_GROUP
    assert seqlen % max(S5_ROWS, HGRN_ROWS, ATT_BLOCK) == 0
    h = x.reshape(batch * seqlen, d_model)
    for layer in range(depth):
        j = layer // 2
        if layer % 2 == 0:
            proj = norm_matmul(h, even_norm[j], even_w_in, j, *PROJ_TILE)
            prep = s5_prep(s5_lambda_re[j], s5_lambda_im[j], s5_log_dt[j], s5_b_re[j], s5_b_im[j])
            params = _s5_layouts(*prep, s5_c_re[j], s5_c_im[j])
            y_a = s5_mixer(proj, batch, params, s5_d[j], s5_w_glu[j], s5_b_glu[j])
            y_b = hgrn_mixer(proj, batch, s5_width, hgrn_lower_bound, hgrn_o_norm[j], j)
            h = proj_residual([y_a, y_b], even_w_out, j, h, *OUT_TILE)
        else:
            qkv = norm_matmul(h, odd_norm[j], odd_w_qkv, j, *PROJ_TILE)
            o = attention(qkv, batch, n_heads, n_kv, q_norm[j], k_norm[j], att_sinks[j])
            h = proj_residual([o], odd_w_out, j, h, *OUT_TILE)
        h = mlp_residual(h, mlp_norm[layer], mlp_w_up, mlp_w_down, layer, *MLP_TILE)
    return h.reshape(batch, seqlen, d_model)
```

```python
import functools
import math

import jax
import jax.numpy as jnp
from jax import lax
from jax.experimental import pallas as pl
from jax.experimental.pallas import tpu as pltpu

F32 = jnp.float32
BF16 = jnp.bfloat16
EPS = 1e-6
LOG2E = math.log2(math.e)

LANES = 128
SUBLANES = 8
VMEM_LIMIT = 56 * 1024 * 1024

S5_GROUP_SIZE = 16
S5_STATE = 64
S5_MIN_DECAY = 1e-4
HGRN_HEAD_DIM = 128
HGRN_CHUNK = 64
ATT_HEAD_DIM = 64
ATT_GROUP = 8
WINDOW = 128
ATT_BLOCK = 128

PROJ_TILE = (2048, 512)
OUT_TILE = (2048, 512)
MLP_TILE = (1024, 512)
NORM_ROWS = 128
S5_ROWS = 512
S5_SEGS = SUBLANES
S5_SEG = S5_ROWS // S5_SEGS
S5_PITCH = S5_SEG + 8
S5_TILE_GROUPS = LANES // S5_GROUP_SIZE
S5_TILE_STATES = S5_TILE_GROUPS * S5_STATE
HGRN_ROWS = 1024
HGRN_STEP_HEADS = 2


def _cparams(*sem):
    return pltpu.CompilerParams(dimension_semantics=sem, vmem_limit_bytes=VMEM_LIMIT)


def _split_bf16(x):
    hi = x.astype(BF16)
    lo = (x - hi.astype(F32)).astype(BF16)
    return hi, lo


def _rmsnorm_rows(h_ref, g_ref, xn_ref):
    rows = h_ref.shape[0]
    g = g_ref[...]

    def body(i, carry):
        r = pl.multiple_of(i * NORM_ROWS, NORM_ROWS)
        x = h_ref[pl.ds(r, NORM_ROWS), :]
        ms = jnp.mean(x * x, axis=-1, keepdims=True)
        xn_ref[pl.ds(r, NORM_ROWS), :] = (x * lax.rsqrt(ms + EPS) * g).astype(BF16)
        return carry

    lax.fori_loop(0, rows // NORM_ROWS, body, 0)


def _prefetched_rows(h_hbm, h_buf, h_sem, consume):
    i = pl.program_id(0)
    j = pl.program_id(1)
    tm = h_buf.shape[0]

    def h_copy(tile):
        return pltpu.make_async_copy(h_hbm.at[pl.ds(tile * tm, tm), :], h_buf, h_sem)

    @pl.when((i == 0) & (j == 0))
    def _():
        h_copy(0).start()

    @pl.when(j == 0)
    def _():
        h_copy(i).wait()
        consume()

    @pl.when((j == 1) & (i + 1 < pl.num_programs(0)))
    def _():
        h_copy(i + 1).start()


def _norm_matmul_kernel(h_hbm, g_ref, w_ref, o_ref, xn_ref, h_buf, h_sem):
    _prefetched_rows(h_hbm, h_buf, h_sem, lambda: _rmsnorm_rows(h_buf, g_ref, xn_ref))
    o_ref[...] = jnp.dot(xn_ref[...], w_ref[...].astype(BF16), preferred_element_type=F32)


def norm_matmul(h, gain, w_stack, layer, tm, tn):
    m, k = h.shape
    n = w_stack.shape[2]
    tm = min(tm, m)
    assert n // tn >= 2
    return pl.pallas_call(
        _norm_matmul_kernel,
        grid=(m // tm, n // tn),
        in_specs=[
            pl.BlockSpec(memory_space=pl.ANY),
            pl.BlockSpec((1, k), lambda i, j: (0, 0)),
            pl.BlockSpec((None, k, tn), lambda i, j: (layer, 0, j)),
        ],
        out_specs=pl.BlockSpec((tm, tn), lambda i, j: (i, j)),
        out_shape=jax.ShapeDtypeStruct((m, n), F32),
        scratch_shapes=[pltpu.VMEM((tm, k), BF16), pltpu.VMEM((tm, k), F32),
                        pltpu.SemaphoreType.DMA(())],
        compiler_params=_cparams("arbitrary", "arbitrary"),
        name="norm_matmul",
    )(h, gain.reshape(1, k), w_stack)


def _proj_residual_kernel(*refs, n_lhs):
    lhs_refs = refs[:n_lhs]
    w_ref, res_ref, o_ref = refs[n_lhs:]
    acc = res_ref[...]
    k0 = 0
    for lhs_ref in lhs_refs:
        kk = lhs_ref.shape[1]
        acc = acc + jnp.dot(lhs_ref[...], w_ref[k0:k0 + kk, :].astype(BF16),
                            preferred_element_type=F32)
        k0 += kk
    o_ref[...] = acc


def proj_residual(lhs_list, w_stack, layer, res, tm, tn):
    m, n = res.shape
    k = w_stack.shape[1]
    tm = min(tm, m)
    in_specs = [pl.BlockSpec((tm, x.shape[1]), lambda i, j: (i, 0)) for x in lhs_list]
    in_specs += [pl.BlockSpec((None, k, tn), lambda i, j: (layer, 0, j)),
                 pl.BlockSpec((tm, tn), lambda i, j: (i, j))]
    return pl.pallas_call(
        functools.partial(_proj_residual_kernel, n_lhs=len(lhs_list)),
        grid=(m // tm, n // tn),
        in_specs=in_specs,
        out_specs=pl.BlockSpec((tm, tn), lambda i, j: (i, j)),
        out_shape=jax.ShapeDtypeStruct((m, n), F32),
        compiler_params=_cparams("parallel", "arbitrary"),
        name="proj_residual",
    )(*lhs_list, w_stack, res)


def _mlp_kernel(h_hbm, g_ref, wu_ref, wd_ref, o_ref, xn_ref, h_buf, h_sem):
    def first_hidden_tile():
        _rmsnorm_rows(h_buf, g_ref, xn_ref)
        o_ref[...] = h_buf[...]

    _prefetched_rows(h_hbm, h_buf, h_sem, first_hidden_tile)
    up = jnp.dot(xn_ref[...], wu_ref[...].astype(BF16), preferred_element_type=F32)
    act = jnp.square(jnp.maximum(up, 0.0)).astype(BF16)
    o_ref[...] += jnp.dot(act, wd_ref[...].astype(BF16), preferred_element_type=F32)


def mlp_residual(h, gain, w_up_stack, w_down_stack, layer, tm, tf):
    m, d = h.shape
    f = w_up_stack.shape[2]
    tm = min(tm, m)
    assert f // tf >= 2
    return pl.pallas_call(
        _mlp_kernel,
        grid=(m // tm, f // tf),
        in_specs=[
            pl.BlockSpec(memory_space=pl.ANY),
            pl.BlockSpec((1, d), lambda i, j: (0, 0)),
            pl.BlockSpec((None, d, tf), lambda i, j: (layer, 0, j)),
            pl.BlockSpec((None, tf, d), lambda i, j: (layer, j, 0)),
        ],
        out_specs=pl.BlockSpec((tm, d), lambda i, j: (i, 0)),
        out_shape=jax.ShapeDtypeStruct((m, d), F32),
        scratch_shapes=[pltpu.VMEM((tm, d), BF16), pltpu.VMEM((tm, d), F32),
                        pltpu.SemaphoreType.DMA(())],
        compiler_params=_cparams("arbitrary", "arbitrary"),
        name="mlp_residual",
    )(h, gain.reshape(1, d), w_up_stack, w_down_stack)


def _s5_prep_kernel(lre_ref, lim_ref, ldt_ref, bre_ref, bim_ref,
                    ar_ref, ai_ref, asr_ref, asi_ref, bbr_ref, bbi_ref):
    lr = jnp.minimum(lre_ref[...], -S5_MIN_DECAY)
    li = lim_ref[...]
    dt = jnp.exp(ldt_ref[...])
    mag = jnp.exp(lr * dt)
    ar = mag * jnp.cos(li * dt)
    ai = mag * jnp.sin(li * dt)
    den = lr * lr + li * li
    zr = ((ar - 1.0) * lr + ai * li) / den
    zi = (ai * lr - (ar - 1.0) * li) / den
    ar_ref[...] = ar
    ai_ref[...] = ai
    pr, pi = ar, ai
    for _ in range(int(math.log2(S5_SEG))):
        pr, pi = pr * pr - pi * pi, 2.0 * pr * pi
    asr_ref[...] = pr
    asi_ref[...] = pi
    for c in range(S5_GROUP_SIZE):
        br = bre_ref[c]
        bi = bim_ref[c]
        bbr_ref[c] = zr * br - zi * bi
        bbi_ref[c] = zr * bi + zi * br


def s5_prep(lam_re, lam_im, log_dt, b_re, b_im):
    g, p = lam_re.shape
    ldt = jnp.broadcast_to(log_dt[:, None], (g, p))
    bre_t = jnp.transpose(b_re, (2, 0, 1))
    bim_t = jnp.transpose(b_im, (2, 0, 1))
    gp = jax.ShapeDtypeStruct((g, p), F32)
    cgp = jax.ShapeDtypeStruct((S5_GROUP_SIZE, g, p), F32)
    return pl.pallas_call(
        _s5_prep_kernel,
        out_shape=(gp, gp, gp, gp, cgp, cgp),
        name="s5_prep",
    )(lam_re, lam_im, ldt, bre_t, bim_t)


def _s5_kernel(u_ref, bblk_ref, ca_ref, cb_ref, ar_ref, ai_ref, asr_ref, asi_ref,
               d_ref, wglu_ref, bglu_ref, o_ref,
               bu_ref, xp_ref, yp_ref, yn_ref, st_ref):
    n_tiles = u_ref.shape[1] // LANES
    half = S5_TILE_STATES // LANES
    slabs = 2 * half

    @pl.when(pl.program_id(1) == 0)
    def _():
        st_ref[...] = jnp.zeros_like(st_ref)

    u = u_ref[...]
    ub = u.astype(BF16)
    for j in range(n_tiles):
        res = jnp.dot(ub[:, j * LANES:(j + 1) * LANES], bblk_ref[j], preferred_element_type=F32)
        for seg in range(S5_SEGS):
            for k in range(slabs):
                bu_ref[j * slabs + k, seg * S5_PITCH:seg * S5_PITCH + S5_SEG, :] = (
                    res[seg * S5_SEG:(seg + 1) * S5_SEG, k * LANES:(k + 1) * LANES])

    row = lax.broadcasted_iota(jnp.int32, (SUBLANES, LANES), 0)

    def tile_scan(j, carry):
        a_r = [ar_ref[j, :, k * LANES:(k + 1) * LANES] for k in range(half)]
        a_i = [ai_ref[j, :, k * LANES:(k + 1) * LANES] for k in range(half)]

        def load(s, k):
            return bu_ref[j * slabs + k, pl.ds(s, S5_SEGS, stride=S5_PITCH), :]

        def step1(s, x):
            xr, xi = x
            nr, ni = [], []
            for k in range(half):
                br = load(s, k)
                bi = load(s, half + k)
                nr.append(a_r[k] * xr[k] - a_i[k] * xi[k] + br)
                ni.append(a_r[k] * xi[k] + a_i[k] * xr[k] + bi)
            return tuple(nr), tuple(ni)

        zero = tuple(jnp.zeros((SUBLANES, LANES), F32) for _ in range(half))
        fr, fi = lax.fori_loop(0, S5_SEG, step1, (zero, zero), unroll=8)

        init_r, init_i, new_r, new_i = [], [], [], []
        for k in range(half):
            sr = asr_ref[j, :, k * LANES:(k + 1) * LANES]
            si = asi_ref[j, :, k * LANES:(k + 1) * LANES]
            pr = st_ref[j, :, k * LANES:(k + 1) * LANES]
            pi = st_ref[j, :, (half + k) * LANES:(half + k + 1) * LANES]
            ir = jnp.where(row == 0, pr, 0.0)
            ii = jnp.where(row == 0, pi, 0.0)
            for seg in range(1, S5_SEGS):
                er = fr[k] + sr * ir - si * ii
                ei = fi[k] + sr * ii + si * ir
                ir = jnp.where(row == seg, pltpu.roll(er, 1, axis=0), ir)
                ii = jnp.where(row == seg, pltpu.roll(ei, 1, axis=0), ii)
            er = fr[k] + sr * ir - si * ii
            ei = fi[k] + sr * ii + si * ir
            init_r.append(ir)
            init_i.append(ii)
            new_r.append(pltpu.roll(er, 1, axis=0))
            new_i.append(pltpu.roll(ei, 1, axis=0))
        for k in range(half):
            st_ref[j, :, k * LANES:(k + 1) * LANES] = new_r[k]
            st_ref[j, :, (half + k) * LANES:(half + k + 1) * LANES] = new_i[k]

        def step2(s2, x):
            xr, xn = x
            outs_r, outs_n = [], []
            for _ in range(2):
                outs_r.append([])
                outs_n.append([])
            for t in range(2):
                s = s2 * 2 + t
                nr, nn = [], []
                for k in range(half):
                    br = load(s, k)
                    bi = load(s, half + k)
                    nr.append(a_r[k] * xr[k] + a_i[k] * xn[k] + br)
                    nn.append(a_r[k] * xn[k] - a_i[k] * xr[k] - bi)
                xr, xn = tuple(nr), tuple(nn)
                outs_r[t] = nr
                outs_n[t] = nn
            r0 = pl.multiple_of(s2 * (2 * S5_SEGS), 2 * S5_SEGS)
            for k in range(half):
                xp_ref[j, pl.ds(r0, 2 * S5_SEGS), k * LANES:(k + 1) * LANES] = (
                    jnp.concatenate([outs_r[0][k], outs_r[1][k]], axis=0).astype(BF16))
                xp_ref[j, pl.ds(r0, 2 * S5_SEGS), (half + k) * LANES:(half + k + 1) * LANES] = (
                    jnp.concatenate([outs_n[0][k], outs_n[1][k]], axis=0).astype(BF16))
            return xr, xn

        neg_i = tuple(-v for v in init_i)
        lax.fori_loop(0, S5_SEG // 2, step2, (tuple(init_r), neg_i), unroll=4)
        return carry

    lax.fori_loop(0, n_tiles, tile_scan, 0)

    for jj in range(n_tiles // 2):
        yy = (jnp.dot(xp_ref[2 * jj], ca_ref[jj], preferred_element_type=F32)
              + jnp.dot(xp_ref[2 * jj + 1], cb_ref[jj], preferred_element_type=F32))
        yp_ref[2 * jj] = yy[:, :LANES]
        yp_ref[2 * jj + 1] = yy[:, LANES:]

    for j in range(n_tiles):
        for seg in range(S5_SEGS):
            yn_ref[seg * S5_SEG:(seg + 1) * S5_SEG, j * LANES:(j + 1) * LANES] = (
                yp_ref[j, pl.ds(seg, S5_SEG, stride=S5_SEGS), :])

    y = yn_ref[...] + d_ref[...] * u
    y = jax.nn.gelu(y)
    z = jnp.dot(y.astype(BF16), wglu_ref[...], preferred_element_type=F32) + bglu_ref[...]
    o_ref[...] = (y * jax.nn.sigmoid(z)).astype(o_ref.dtype)


def _s5_layouts(ar, ai, asr, asi, bbr, bbi, c_re, c_im):
    g, p = ar.shape
    tg = S5_TILE_GROUPS
    nt = g // tg
    eye = jnp.eye(tg, dtype=F32)

    def lanes(a):
        return jnp.broadcast_to(a.reshape(nt, 1, tg * p), (nt, SUBLANES, tg * p))

    bb = jnp.stack([bbr, bbi]).reshape(2, S5_GROUP_SIZE, nt, tg, p)
    bb = jnp.transpose(bb, (2, 3, 1, 0, 4))
    bblk = bb[:, :, :, :, None, :] * eye[None, :, None, None, :, None]
    bblk = bblk.reshape(nt, tg * S5_GROUP_SIZE, 2 * tg * p).astype(BF16)

    cc = jnp.stack([c_re, c_im]).reshape(2, nt, tg, S5_GROUP_SIZE, p)
    cc = jnp.transpose(cc, (1, 0, 2, 4, 3))
    cblk = cc[:, :, :, :, None, :] * eye[None, None, :, None, :, None]
    cblk = cblk.reshape(nt, 2 * tg * p, tg * S5_GROUP_SIZE)
    zeros = jnp.zeros_like(cblk[0::2])
    ca = jnp.concatenate([cblk[0::2], zeros], axis=-1).astype(BF16)
    cb = jnp.concatenate([zeros, cblk[1::2]], axis=-1).astype(BF16)
    return bblk, ca, cb, lanes(ar), lanes(ai), lanes(asr), lanes(asi)


def s5_mixer(proj, batch, params, d_skip, w_glu, b_glu):
    bblk, ca, cb, ar, ai, asr, asi = params
    m = proj.shape[0]
    width = w_glu.shape[0]
    nt = width // LANES
    steps = m // batch // S5_ROWS
    slabs = 2 * S5_TILE_STATES // LANES
    const3 = lambda b, n: (0, 0, 0)
    const2 = lambda b, n: (0, 0)
    once = pl.Buffered(1)
    return pl.pallas_call(
        _s5_kernel,
        grid=(batch, steps),
        in_specs=[
            pl.BlockSpec((S5_ROWS, width), lambda b, n: (b * steps + n, 0)),
            pl.BlockSpec(bblk.shape, const3, pipeline_mode=once),
            pl.BlockSpec(ca.shape, const3, pipeline_mode=once),
            pl.BlockSpec(cb.shape, const3, pipeline_mode=once),
            pl.BlockSpec(ar.shape, const3),
            pl.BlockSpec(ai.shape, const3),
            pl.BlockSpec(asr.shape, const3),
            pl.BlockSpec(asi.shape, const3),
            pl.BlockSpec((1, width), const2),
            pl.BlockSpec((width, width), const2, pipeline_mode=once),
            pl.BlockSpec((1, width), const2),
        ],
        out_specs=pl.BlockSpec((S5_ROWS, width), lambda b, n: (b * steps + n, 0)),
        out_shape=jax.ShapeDtypeStruct((m, width), BF16),
        scratch_shapes=[
            pltpu.VMEM((nt * slabs, S5_SEGS * S5_PITCH, LANES), F32),
            pltpu.VMEM((nt, S5_ROWS, 2 * S5_TILE_STATES), BF16),
            pltpu.VMEM((nt, S5_ROWS, LANES), F32),
            pltpu.VMEM((S5_ROWS, width), F32),
            pltpu.VMEM((nt, SUBLANES, 2 * S5_TILE_STATES), F32),
        ],
        compiler_params=_cparams("arbitrary", "arbitrary"),
        name="s5_mixer",
    )(proj, bblk, ca, cb, ar, ai, asr, asi, d_skip.reshape(1, width),
      w_glu.astype(BF16), b_glu.reshape(1, width))


def _hgrn_kernel(q_ref, f_ref, i_ref, g_ref, lbp_ref, og_ref, cum_ref, o_ref, st_ref, *, layer):
    @pl.when(pl.program_id(2) == 0)
    def _():
        st_ref[...] = jnp.zeros_like(st_ref)

    for hh in range(st_ref.shape[0]):
        lanes = pl.ds(hh * HGRN_HEAD_DIM, HGRN_HEAD_DIM)
        _hgrn_head(q_ref.at[:, lanes], f_ref.at[:, lanes], i_ref.at[:, lanes], g_ref.at[:, lanes],
                   lbp_ref.at[:, lanes], og_ref, cum_ref, o_ref.at[:, lanes], st_ref.at[hh], layer)


def _hgrn_head(q_ref, f_ref, i_ref, g_ref, lbp_ref, og_ref, cum_ref, o_ref, st_ref, layer):
    n_levels = int(math.log2(HGRN_CHUNK))
    c = HGRN_CHUNK
    hd = HGRN_HEAD_DIM

    lbp = lbp_ref[...]
    e = jnp.exp(lbp - jnp.max(lbp, axis=0, keepdims=True))
    prob = e / jnp.sum(e, axis=0, keepdims=True)
    lb = jnp.sum(prob[:layer + 1], axis=0, keepdims=True) - prob[0:1]
    og = og_ref[...]

    rows = q_ref.shape[0]
    n_chunks = rows // c
    ti = lax.broadcasted_iota(jnp.int32, (c, c), 0)
    si = lax.broadcasted_iota(jnp.int32, (c, c), 1)
    diff_bits = ti ^ si
    tril = cum_ref[...]
    tmod = lax.broadcasted_iota(jnp.int32, (rows, hd), 0) & (c - 1)
    nt_dims = (((1,), (1,)), ((), ()))

    def chunk_rows(x, ch):
        return x[ch * c:(ch + 1) * c]

    forget = lb + (1.0 - lb) * jax.nn.sigmoid(f_ref[...])
    logf = jnp.log2(forget)
    kk = 1.0 - forget
    qq = jax.nn.silu(q_ref[...])
    vb = i_ref[...].astype(BF16)

    hi, lo = _split_bf16(logf)
    b = jnp.concatenate(
        [jnp.dot(tril, chunk_rows(hi, ch), preferred_element_type=F32)
         + jnp.dot(tril, chunk_rows(lo, ch), preferred_element_type=F32) for ch in range(n_chunks)],
        axis=0)

    block_end = b
    scores = [jnp.zeros((c, c), F32) for _ in range(n_chunks)]
    for l in range(n_levels):
        blk = 1 << l
        prev_end = pltpu.roll(block_end, blk, axis=0)
        eq = b - jnp.where(tmod < blk, 0.0, prev_end)
        ek = block_end - b
        ql = (qq * jnp.exp2(eq)).astype(BF16)
        kl = (kk * jnp.exp2(ek)).astype(BF16)
        mask = ((diff_bits >> l) == 1) & (ti > si)
        for ch in range(n_chunks):
            sl = lax.dot_general(chunk_rows(ql, ch), chunk_rows(kl, ch), nt_dims,
                                 preferred_element_type=F32)
            scores[ch] = jnp.where(mask, sl, scores[ch])
        next_end = pltpu.roll(block_end, rows - blk, axis=0)
        block_end = jnp.where((tmod & blk) == 0, next_end, block_end)

    dsum = jnp.sum(qq * kk, axis=-1, keepdims=True)
    qe = (qq * jnp.exp2(b)).astype(BF16)
    kd = (kk * jnp.exp2(block_end - b)).astype(BF16)
    chunk_decay = jnp.exp2(block_end)

    st = st_ref[...]
    outs = []
    for ch in range(n_chunks):
        sc = jnp.where(ti == si, chunk_rows(dsum, ch), scores[ch]).astype(BF16)
        v_c = chunk_rows(vb, ch)
        outs.append(jnp.dot(sc, v_c, preferred_element_type=F32)
                    + lax.dot_general(chunk_rows(qe, ch), st.astype(BF16), nt_dims,
                                      preferred_element_type=F32))
        st = (st * chunk_decay[ch * c:ch * c + 1]
              + lax.dot_general(v_c, chunk_rows(kd, ch), (((0,), (0,)), ((), ())),
                                preferred_element_type=F32))
    st_ref[...] = st

    out = jnp.concatenate(outs, axis=0)
    ms = jnp.mean(out * out, axis=-1, keepdims=True)
    o_ref[...] = (out * lax.rsqrt(ms + EPS) * og * jax.nn.silu(g_ref[...])).astype(o_ref.dtype)


def _hgrn_tril():
    c = HGRN_CHUNK
    return (jnp.arange(c)[None, :] <= jnp.arange(c)[:, None]).astype(BF16)


def hgrn_mixer(proj, batch, col0, lb_param, o_gain, layer):
    m = proj.shape[0]
    n_layers, width = lb_param.shape
    heads = width // HGRN_HEAD_DIM
    steps = m // batch // HGRN_ROWS
    hd = HGRN_HEAD_DIM
    cum = _hgrn_tril()

    hw = HGRN_STEP_HEADS * hd

    def col(block):
        return pl.BlockSpec((HGRN_ROWS, hw),
                            lambda b, h, n: (b * steps + n, (col0 + block * width) // hw + h))

    return pl.pallas_call(
        functools.partial(_hgrn_kernel, layer=layer),
        grid=(batch, heads // HGRN_STEP_HEADS, steps),
        in_specs=[
            col(0), col(1), col(2), col(3),
            pl.BlockSpec((n_layers, hw), lambda b, h, n: (0, h)),
            pl.BlockSpec((1, hd), lambda b, h, n: (0, 0)),
            pl.BlockSpec(cum.shape, lambda b, h, n: (0, 0)),
        ],
        out_specs=pl.BlockSpec((HGRN_ROWS, hw), lambda b, h, n: (b * steps + n, h)),
        out_shape=jax.ShapeDtypeStruct((m, width), BF16),
        scratch_shapes=[pltpu.VMEM((HGRN_STEP_HEADS, hd, hd), F32)],
        compiler_params=_cparams("arbitrary", "arbitrary", "arbitrary"),
        name="hgrn_mixer",
    )(proj, proj, proj, proj, lb_param, o_gain.reshape(1, hd), cum)


def _attn_kernel(sink_ref, slope_ref, q_ref, kc_ref, vc_ref, kp_ref, vp_ref,
                 qg_ref, kg_ref, ind_ref, indt_ref, o_ref, bias_ref, *, n_kv):
    blk = ATT_BLOCK
    hd = ATT_HEAD_DIM
    nblk = pl.program_id(1)
    ind = ind_ref[...]
    indt = indt_ref[...]

    def head_rms_scale(x):
        w = x.shape[1]
        hi, lo = _split_bf16(x * x)
        ss = (jnp.dot(hi, ind[:w], preferred_element_type=F32)
              + jnp.dot(lo, ind[:w], preferred_element_type=F32))
        rs = lax.rsqrt(ss * (1.0 / hd) + EPS)
        hi, lo = _split_bf16(rs)
        return (jnp.dot(hi, indt[:, :w], preferred_element_type=F32)
                + jnp.dot(lo, indt[:, :w], preferred_element_type=F32))

    @pl.when(nblk <= 1)
    def _():
        t = lax.broadcasted_iota(jnp.int32, (blk, 2 * blk), 0)
        s = lax.broadcasted_iota(jnp.int32, (blk, 2 * blk), 1)
        dist = t + blk - s
        valid = (dist >= 0) & (dist < WINDOW) & ((s >= blk) | (nblk > 0))
        negdist = -dist.astype(F32)
        for h in range(bias_ref.shape[0]):
            tab = jnp.where(valid, (slope_ref[h] * LOG2E) * negdist, -jnp.inf)
            bias_ref[h] = jnp.where(s == 0, sink_ref[h] * LOG2E, tab)

    q = q_ref[...]
    qn = (q * head_rms_scale(q) * qg_ref[...] * (hd ** -0.5 * LOG2E)).astype(BF16)
    k = jnp.concatenate([kp_ref[...], kc_ref[...]], axis=0)
    kn = k * head_rms_scale(k) * kg_ref[...]
    v = jnp.concatenate([vp_ref[...], vc_ref[...]], axis=0)
    lane = lax.broadcasted_iota(jnp.int32, (2 * blk, LANES), 1)
    key_row = lax.broadcasted_iota(jnp.int32, (2 * blk, LANES), 0)
    low = lax.broadcasted_iota(jnp.int32, (blk, LANES), 1) < hd

    tiles = ATT_GROUP // 2
    for kv in range(n_kv):
        tile = kv // 2
        k_t = kn[:, tile * LANES:(tile + 1) * LANES]
        v_t = v[:, tile * LANES:(tile + 1) * LANES]
        own = ((lane < hd) if kv % 2 == 0 else (lane >= hd)) & (key_row > 0)
        k_own = jnp.where(own, k_t, 0.0)
        v_own = jnp.where(own, v_t, 0.0)
        k_half = [None, None]
        v_half = [None, None]
        k_half[kv % 2] = k_own
        v_half[kv % 2] = v_own
        k_half[1 - kv % 2] = pltpu.roll(k_own, hd, axis=1)
        v_half[1 - kv % 2] = pltpu.roll(v_own, hd, axis=1)
        for par in range(2):
            k_half[par] = k_half[par].astype(BF16)
            data_half = (lane < hd) if par == 0 else (lane >= hd)
            v_half[par] = jnp.where(data_half, v_half[par], 1.0).astype(BF16)
        for mt in range(tiles):
            qt = kv * tiles + mt
            q_t = qn[:, qt * LANES:(qt + 1) * LANES]
            pv = []
            for par in range(2):
                sc = lax.dot_general(q_t, k_half[par], (((1,), (1,)), ((), ())),
                                     preferred_element_type=F32) + bias_ref[2 * qt + par]
                mx = jnp.max(sc, axis=-1, keepdims=True)
                p = jnp.exp2(sc - mx).astype(BF16)
                pv.append(jnp.dot(p, v_half[par], preferred_element_type=F32))
            num = jnp.where(low, pv[0], pv[1])
            den = pltpu.roll(jnp.where(low, pv[1], pv[0]), hd, axis=1)
            o_ref[:, qt * LANES:(qt + 1) * LANES] = (num / den).astype(o_ref.dtype)


def attention(qkv, batch, n_heads, n_kv, q_gain, k_gain, sinks):
    m = qkv.shape[0]
    hd = ATT_HEAD_DIM
    qw = n_heads * hd
    kw = n_kv * hd
    steps = m // batch // ATT_BLOCK
    slopes = jnp.exp2(-8.0 * jnp.arange(1, n_heads + 1, dtype=F32) / n_heads)
    head_of_lane = jnp.arange(qw) // hd
    ind = (head_of_lane[:, None] == jnp.arange(LANES)[None, :]).astype(BF16)
    cur = lambda b, n, *_: b * steps + n
    prev = lambda b, n, *_: b * steps + jnp.maximum(n - 1, 0)
    grid_spec = pltpu.PrefetchScalarGridSpec(
        num_scalar_prefetch=2,
        grid=(batch, steps),
        in_specs=[
            pl.BlockSpec((ATT_BLOCK, qw), lambda b, n, *_: (cur(b, n), 0)),
            pl.BlockSpec((ATT_BLOCK, kw), lambda b, n, *_: (cur(b, n), qw // kw)),
            pl.BlockSpec((ATT_BLOCK, kw), lambda b, n, *_: (cur(b, n), qw // kw + 1)),
            pl.BlockSpec((ATT_BLOCK, kw), lambda b, n, *_: (prev(b, n), qw // kw)),
            pl.BlockSpec((ATT_BLOCK, kw), lambda b, n, *_: (prev(b, n), qw // kw + 1)),
            pl.BlockSpec((1, qw), lambda b, n, *_: (0, 0)),
            pl.BlockSpec((1, kw), lambda b, n, *_: (0, 0)),
            pl.BlockSpec(ind.shape, lambda b, n, *_: (0, 0)),
            pl.BlockSpec(ind.shape[::-1], lambda b, n, *_: (0, 0)),
        ],
        out_specs=pl.BlockSpec((ATT_BLOCK, qw), lambda b, n, *_: (cur(b, n), 0)),
        scratch_shapes=[pltpu.VMEM((n_heads, ATT_BLOCK, 2 * ATT_BLOCK), F32)],
    )
    return pl.pallas_call(
        functools.partial(_attn_kernel, n_kv=n_kv),
        grid_spec=grid_spec,
        out_shape=jax.ShapeDtypeStruct((m, qw), BF16),
        compiler_params=_cparams("arbitrary", "arbitrary"),
        name="swa_attention",
    )(sinks, slopes, qkv, qkv, qkv, qkv, qkv,
      jnp.tile(q_gain, n_heads).reshape(1, qw), jnp.tile(k_gain, n_kv).reshape(1, kw),
      ind, ind.T)


def kernel(x, even_norm, even_w_in, s5_lambda_re, s5_lambda_im, s5_log_dt, s5_b_re, s5_b_im,
           s5_c_re, s5_c_im, s5_d, s5_w_glu, s5_b_glu, hgrn_lower_bound, hgrn_o_norm, even_w_out,
           odd_norm, odd_w_qkv, q_norm, k_norm, att_sinks, odd_w_out, mlp_norm, mlp_w_up, mlp_w_down):
    batch, seqlen, d_model = x.shape
    depth = mlp_norm.shape[0]
    s5_width = s5_w_glu.shape[1]
    n_heads = att_sinks.shape[1]
    n_kv = n_heads // ATT_GROUP
    assert seqlen % max(S5_ROWS, HGRN_ROWS, ATT_BLOCK) == 0
    h = x.reshape(batch * seqlen, d_model)
    for layer in range(depth):
        j = layer // 2
        if layer % 2 == 0:
            proj = norm_matmul(h, even_norm[j], even_w_in, j, *PROJ_TILE)
            prep = s5_prep(s5_lambda_re[j], s5_lambda_im[j], s5_log_dt[j], s5_b_re[j], s5_b_im[j])
            params = _s5_layouts(*prep, s5_c_re[j], s5_c_im[j])
            y_a = s5_mixer(proj, batch, params, s5_d[j], s5_w_glu[j], s5_b_glu[j])
            y_b = hgrn_mixer(proj, batch, s5_width, hgrn_lower_bound, hgrn_o_norm[j], j)
            h = proj_residual([y_a, y_b], even_w_out, j, h, *OUT_TILE)
        else:
            qkv = norm_matmul(h, odd_norm[j], odd_w_qkv, j, *PROJ_TILE)
            o = attention(qkv, batch, n_heads, n_kv, q_norm[j], k_norm[j], att_sinks[j])
            h = proj_residual([o], odd_w_out, j, h, *OUT_TILE)
        h = mlp_residual(h, mlp_norm[layer], mlp_w_up, mlp_w_down, layer, *MLP_TILE)
    return h.reshape(batch, seqlen, d_model)
```

```python
import functools
import math

import jax
import jax.numpy as jnp
from jax import lax
from jax.experimental import pallas as pl
from jax.experimental.pallas import tpu as pltpu

F32 = jnp.float32
BF16 = jnp.bfloat16
EPS = 1e-6
LOG2E = math.log2(math.e)

LANES = 128
SUBLANES = 8
VMEM_LIMIT = 56 * 1024 * 1024

S5_GROUP_SIZE = 16
S5_STATE = 64
S5_MIN_DECAY = 1e-4
HGRN_HEAD_DIM = 128
HGRN_CHUNK = 64
ATT_HEAD_DIM = 64
ATT_GROUP = 8
WINDOW = 128
ATT_BLOCK = 128

PROJ_TILE = (2048, 512)
OUT_TILE = (2048, 512)
MLP_TILE = (1024, 512)
NORM_ROWS = 128
S5_ROWS = 512
S5_SEGS = SUBLANES
S5_SEG = S5_ROWS // S5_SEGS
S5_PITCH = S5_SEG + 8
S5_TILE_GROUPS = LANES // S5_GROUP_SIZE
S5_TILE_STATES = S5_TILE_GROUPS * S5_STATE
HGRN_ROWS = 1024
HGRN_STEP_HEADS = 2


def _cparams(*sem):
    return pltpu.CompilerParams(dimension_semantics=sem, vmem_limit_bytes=VMEM_LIMIT)


def _split_bf16(x):
    hi = x.astype(BF16)
    lo = (x - hi.astype(F32)).astype(BF16)
    return hi, lo


def _rmsnorm_rows(h_ref, g_ref, xn_ref):
    rows = h_ref.shape[0]
    g = g_ref[...]

    def body(i, carry):
        r = pl.multiple_of(i * NORM_ROWS, NORM_ROWS)
        x = h_ref[pl.ds(r, NORM_ROWS), :]
        ms = jnp.mean(x * x, axis=-1, keepdims=True)
        xn_ref[pl.ds(r, NORM_ROWS), :] = (x * lax.rsqrt(ms + EPS) * g).astype(BF16)
        return carry

    lax.fori_loop(0, rows // NORM_ROWS, body, 0)


def _prefetched_rows(h_hbm, h_buf, h_sem, consume):
    i = pl.program_id(0)
    j = pl.program_id(1)
    tm = h_buf.shape[0]

    def h_copy(tile):
        return pltpu.make_async_copy(h_hbm.at[pl.ds(tile * tm, tm), :], h_buf, h_sem)

    @pl.when((i == 0) & (j == 0))
    def _():
        h_copy(0).start()

    @pl.when(j == 0)
    def _():
        h_copy(i).wait()
        consume()

    @pl.when((j == 1) & (i + 1 < pl.num_programs(0)))
    def _():
        h_copy(i + 1).start()


def _norm_matmul_kernel(h_hbm, g_ref, w_ref, o_ref, xn_ref, h_buf, h_sem):
    _prefetched_rows(h_hbm, h_buf, h_sem, lambda: _rmsnorm_rows(h_buf, g_ref, xn_ref))
    o_ref[...] = jnp.dot(xn_ref[...], w_ref[...].astype(BF16), preferred_element_type=F32)


def norm_matmul(h, gain, w_stack, layer, tm, tn):
    m, k = h.shape
    n = w_stack.shape[2]
    tm = min(tm, m)
    assert n // tn >= 2
    return pl.pallas_call(
        _norm_matmul_kernel,
        grid=(m // tm, n // tn),
        in_specs=[
            pl.BlockSpec(memory_space=pl.ANY),
            pl.BlockSpec((1, k), lambda i, j: (0, 0)),
            pl.BlockSpec((None, k, tn), lambda i, j: (layer, 0, j)),
        ],
        out_specs=pl.BlockSpec((tm, tn), lambda i, j: (i, j)),
        out_shape=jax.ShapeDtypeStruct((m, n), F32),
        scratch_shapes=[pltpu.VMEM((tm, k), BF16), pltpu.VMEM((tm, k), F32),
                        pltpu.SemaphoreType.DMA(())],
        compiler_params=_cparams("arbitrary", "arbitrary"),
        name="norm_matmul",
    )(h, gain.reshape(1, k), w_stack)


def _proj_residual_kernel(*refs, n_lhs):
    lhs_refs = refs[:n_lhs]
    w_ref, res_ref, o_ref = refs[n_lhs:]
    acc = res_ref[...]
    k0 = 0
    for lhs_ref in lhs_refs:
        kk = lhs_ref.shape[1]
        acc = acc + jnp.dot(lhs_ref[...], w_ref[k0:k0 + kk, :].astype(BF16),
                            preferred_element_type=F32)
        k0 += kk
    o_ref[...] = acc


def proj_residual(lhs_list, w_stack, layer, res, tm, tn):
    m, n = res.shape
    k = w_stack.shape[1]
    tm = min(tm, m)
    in_specs = [pl.BlockSpec((tm, x.shape[1]), lambda i, j: (i, 0)) for x in lhs_list]
    in_specs += [pl.BlockSpec((None, k, tn), lambda i, j: (layer, 0, j)),
                 pl.BlockSpec((tm, tn), lambda i, j: (i, j))]
    return pl.pallas_call(
        functools.partial(_proj_residual_kernel, n_lhs=len(lhs_list)),
        grid=(m // tm, n // tn),
        in_specs=in_specs,
        out_specs=pl.BlockSpec((tm, tn), lambda i, j: (i, j)),
        out_shape=jax.ShapeDtypeStruct((m, n), F32),
        compiler_params=_cparams("parallel", "arbitrary"),
        name="proj_residual",
    )(*lhs_list, w_stack, res)


def _mlp_kernel(h_hbm, g_ref, wu_ref, wd_ref, o_ref, xn_ref, h_buf, h_sem):
    def first_hidden_tile():
        _rmsnorm_rows(h_buf, g_ref, xn_ref)
        o_ref[...] = h_buf[...]

    _prefetched_rows(h_hbm, h_buf, h_sem, first_hidden_tile)
    up = jnp.dot(xn_ref[...], wu_ref[...].astype(BF16), preferred_element_type=F32)
    act = jnp.square(jnp.maximum(up, 0.0)).astype(BF16)
    o_ref[...] += jnp.dot(act, wd_ref[...].astype(BF16), preferred_element_type=F32)


def mlp_residual(h, gain, w_up_stack, w_down_stack, layer, tm, tf):
    m, d = h.shape
    f = w_up_stack.shape[2]
    tm = min(tm, m)
    assert f // tf >= 2
    return pl.pallas_call(
        _mlp_kernel,
        grid=(m // tm, f // tf),
        in_specs=[
            pl.BlockSpec(memory_space=pl.ANY),
            pl.BlockSpec((1, d), lambda i, j: (0, 0)),
            pl.BlockSpec((None, d, tf), lambda i, j: (layer, 0, j)),
            pl.BlockSpec((None, tf, d), lambda i, j: (layer, j, 0)),
        ],
        out_specs=pl.BlockSpec((tm, d), lambda i, j: (i, 0)),
        out_shape=jax.ShapeDtypeStruct((m, d), F32),
        scratch_shapes=[pltpu.VMEM((tm, d), BF16), pltpu.VMEM((tm, d), F32),
                        pltpu.SemaphoreType.DMA(())],
        compiler_params=_cparams("arbitrary", "arbitrary"),
        name="mlp_residual",
    )(h, gain.reshape(1, d), w_up_stack, w_down_stack)


def _s5_prep_kernel(lre_ref, lim_ref, ldt_ref, bre_ref, bim_ref,
                    ar_ref, ai_ref, asr_ref, asi_ref, bbr_ref, bbi_ref):
    lr = jnp.minimum(lre_ref[...], -S5_MIN_DECAY)
    li = lim_ref[...]
    dt = jnp.exp(ldt_ref[...])
    mag = jnp.exp(lr * dt)
    ar = mag * jnp.cos(li * dt)
    ai = mag * jnp.sin(li * dt)
    den = lr * lr + li * li
    zr = ((ar - 1.0) * lr + ai * li) / den
    zi = (ai * lr - (ar - 1.0) * li) / den
    ar_ref[...] = ar
    ai_ref[...] = ai
    pr, pi = ar, ai
    for _ in range(int(math.log2(S5_SEG))):
        pr, pi = pr * pr - pi * pi, 2.0 * pr * pi
    asr_ref[...] = pr
    asi_ref[...] = pi
    for c in range(S5_GROUP_SIZE):
        br = bre_ref[c]
        bi = bim_ref[c]
        bbr_ref[c] = zr * br - zi * bi
        bbi_ref[c] = zr * bi + zi * br


def s5_prep(lam_re, lam_im, log_dt, b_re, b_im):
    g, p = lam_re.shape
    ldt = jnp.broadcast_to(log_dt[:, None], (g, p))
    bre_t = jnp.transpose(b_re, (2, 0, 1))
    bim_t = jnp.transpose(b_im, (2, 0, 1))
    gp = jax.ShapeDtypeStruct((g, p), F32)
    cgp = jax.ShapeDtypeStruct((S5_GROUP_SIZE, g, p), F32)
    return pl.pallas_call(
        _s5_prep_kernel,
        out_shape=(gp, gp, gp, gp, cgp, cgp),
        name="s5_prep",
    )(lam_re, lam_im, ldt, bre_t, bim_t)


def _s5_kernel(u_ref, bblk_ref, ca_ref, cb_ref, ar_ref, ai_ref, asr_ref, asi_ref,
               d_ref, wglu_ref, bglu_ref, o_ref,
               bu_ref, xp_ref, yp_ref, yn_ref, st_ref):
    n_tiles = u_ref.shape[1] // LANES
    half = S5_TILE_STATES // LANES
    slabs = 2 * half

    @pl.when(pl.program_id(1) == 0)
    def _():
        st_ref[...] = jnp.zeros_like(st_ref)

    u = u_ref[...]
    ub = u.astype(BF16)
    for j in range(n_tiles):
        res = jnp.dot(ub[:, j * LANES:(j + 1) * LANES], bblk_ref[j], preferred_element_type=F32)
        for seg in range(S5_SEGS):
            for k in range(slabs):
                bu_ref[j * slabs + k, seg * S5_PITCH:seg * S5_PITCH + S5_SEG, :] = (
                    res[seg * S5_SEG:(seg + 1) * S5_SEG, k * LANES:(k + 1) * LANES])

    row = lax.broadcasted_iota(jnp.int32, (SUBLANES, LANES), 0)

    def tile_scan(j, carry):
        a_r = [ar_ref[j, :, k * LANES:(k + 1) * LANES] for k in range(half)]
        a_i = [ai_ref[j, :, k * LANES:(k + 1) * LANES] for k in range(half)]

        def load(s, k):
            return bu_ref[j * slabs + k, pl.ds(s, S5_SEGS, stride=S5_PITCH), :]

        def step1(s, x):
            xr, xi = x
            nr, ni = [], []
            for k in range(half):
                br = load(s, k)
                bi = load(s, half + k)
                nr.append(a_r[k] * xr[k] - a_i[k] * xi[k] + br)
                ni.append(a_r[k] * xi[k] + a_i[k] * xr[k] + bi)
            return tuple(nr), tuple(ni)

        zero = tuple(jnp.zeros((SUBLANES, LANES), F32) for _ in range(half))
        fr, fi = lax.fori_loop(0, S5_SEG, step1, (zero, zero), unroll=8)

        init_r, init_i, new_r, new_i = [], [], [], []
        for k in range(half):
            sr = asr_ref[j, :, k * LANES:(k + 1) * LANES]
            si = asi_ref[j, :, k * LANES:(k + 1) * LANES]
            pr = st_ref[j, :, k * LANES:(k + 1) * LANES]
            pi = st_ref[j, :, (half + k) * LANES:(half + k + 1) * LANES]
            ir = jnp.where(row == 0, pr, 0.0)
            ii = jnp.where(row == 0, pi, 0.0)
            for seg in range(1, S5_SEGS):
                er = fr[k] + sr * ir - si * ii
                ei = fi[k] + sr * ii + si * ir
                ir = jnp.where(row == seg, pltpu.roll(er, 1, axis=0), ir)
                ii = jnp.where(row == seg, pltpu.roll(ei, 1, axis=0), ii)
            er = fr[k] + sr * ir - si * ii
            ei = fi[k] + sr * ii + si * ir
            init_r.append(ir)
            init_i.append(ii)
            new_r.append(pltpu.roll(er, 1, axis=0))
            new_i.append(pltpu.roll(ei, 1, axis=0))
        for k in range(half):
            st_ref[j, :, k * LANES:(k + 1) * LANES] = new_r[k]
            st_ref[j, :, (half + k) * LANES:(half + k + 1) * LANES] = new_i[k]

        def step2(s2, x):
            xr, xn = x
            outs_r, outs_n = [], []
            for _ in range(2):
                outs_r.append([])
                outs_n.append([])
            for t in range(2):
                s = s2 * 2 + t
                nr, nn = [], []
                for k in range(half):
                    br = load(s, k)
                    bi = load(s, half + k)
                    nr.append(a_r[k] * xr[k] + a_i[k] * xn[k] + br)
                    nn.append(a_r[k] * xn[k] - a_i[k] * xr[k] - bi)
                xr, xn = tuple(nr), tuple(nn)
                outs_r[t] = nr
                outs_n[t] = nn
            r0 = pl.multiple_of(s2 * (2 * S5_SEGS), 2 * S5_SEGS)
            for k in range(half):
                xp_ref[j, pl.ds(r0, 2 * S5_SEGS), k * LANES:(k + 1) * LANES] = (
                    jnp.concatenate([outs_r[0][k], outs_r[1][k]], axis=0).astype(BF16))
                xp_ref[j, pl.ds(r0, 2 * S5_SEGS), (half + k) * LANES:(half + k + 1) * LANES] = (
                    jnp.concatenate([outs_n[0][k], outs_n[1][k]], axis=0).astype(BF16))
            return xr, xn

        neg_i = tuple(-v for v in init_i)
        lax.fori_loop(0, S5_SEG // 2, step2, (tuple(init_r), neg_i), unroll=4)
        return carry

    lax.fori_loop(0, n_tiles, tile_scan, 0)

    for jj in range(n_tiles // 2):
        yy = (jnp.dot(xp_ref[2 * jj], ca_ref[jj], preferred_element_type=F32)
              + jnp.dot(xp_ref[2 * jj + 1], cb_ref[jj], preferred_element_type=F32))
        yp_ref[2 * jj] = yy[:, :LANES]
        yp_ref[2 * jj + 1] = yy[:, LANES:]

    for j in range(n_tiles):
        for seg in range(S5_SEGS):
            yn_ref[seg * S5_SEG:(seg + 1) * S5_SEG, j * LANES:(j + 1) * LANES] = (
                yp_ref[j, pl.ds(seg, S5_SEG, stride=S5_SEGS), :])

    y = yn_ref[...] + d_ref[...] * u
    y = jax.nn.gelu(y)
    z = jnp.dot(y.astype(BF16), wglu_ref[...], preferred_element_type=F32) + bglu_ref[...]
    o_ref[...] = (y * jax.nn.sigmoid(z)).astype(o_ref.dtype)


def _s5_layouts(ar, ai, asr, asi, bbr, bbi, c_re, c_im):
    g, p = ar.shape
    tg = S5_TILE_GROUPS
    nt = g // tg
    eye = jnp.eye(tg, dtype=F32)

    def lanes(a):
        return jnp.broadcast_to(a.reshape(nt, 1, tg * p), (nt, SUBLANES, tg * p))

    bb = jnp.stack([bbr, bbi]).reshape(2, S5_GROUP_SIZE, nt, tg, p)
    bb = jnp.transpose(bb, (2, 3, 1, 0, 4))
    bblk = bb[:, :, :, :, None, :] * eye[None, :, None, None, :, None]
    bblk = bblk.reshape(nt, tg * S5_GROUP_SIZE, 2 * tg * p).astype(BF16)

    cc = jnp.stack([c_re, c_im]).reshape(2, nt, tg, S5_GROUP_SIZE, p)
    cc = jnp.transpose(cc, (1, 0, 2, 4, 3))
    cblk = cc[:, :, :, :, None, :] * eye[None, None, :, None, :, None]
    cblk = cblk.reshape(nt, 2 * tg * p, tg * S5_GROUP_SIZE)
    zeros = jnp.zeros_like(cblk[0::2])
    ca = jnp.concatenate([cblk[0::2], zeros], axis=-1).astype(BF16)
    cb = jnp.concatenate([zeros, cblk[1::2]], axis=-1).astype(BF16)
    return bblk, ca, cb, lanes(ar), lanes(ai), lanes(asr), lanes(asi)


def s5_mixer(proj, batch, params, d_skip, w_glu, b_glu):
    bblk, ca, cb, ar, ai, asr, asi = params
    m = proj.shape[0]
    width = w_glu.shape[0]
    nt = width // LANES
    steps = m // batch // S5_ROWS
    slabs = 2 * S5_TILE_STATES // LANES
    const3 = lambda b, n: (0, 0, 0)
    const2 = lambda b, n: (0, 0)
    once = pl.Buffered(1)
    return pl.pallas_call(
        _s5_kernel,
        grid=(batch, steps),
        in_specs=[
            pl.BlockSpec((S5_ROWS, width), lambda b, n: (b * steps + n, 0)),
            pl.BlockSpec(bblk.shape, const3, pipeline_mode=once),
            pl.BlockSpec(ca.shape, const3, pipeline_mode=once),
            pl.BlockSpec(cb.shape, const3, pipeline_mode=once),
            pl.BlockSpec(ar.shape, const3),
            pl.BlockSpec(ai.shape, const3),
            pl.BlockSpec(asr.shape, const3),
            pl.BlockSpec(asi.shape, const3),
            pl.BlockSpec((1, width), const2),
            pl.BlockSpec((width, width), const2, pipeline_mode=once),
            pl.BlockSpec((1, width), const2),
        ],
        out_specs=pl.BlockSpec((S5_ROWS, width), lambda b, n: (b * steps + n, 0)),
        out_shape=jax.ShapeDtypeStruct((m, width), BF16),
        scratch_shapes=[
            pltpu.VMEM((nt * slabs, S5_SEGS * S5_PITCH, LANES), F32),
            pltpu.VMEM((nt, S5_ROWS, 2 * S5_TILE_STATES), BF16),
            pltpu.VMEM((nt, S5_ROWS, LANES), F32),
            pltpu.VMEM((S5_ROWS, width), F32),
            pltpu.VMEM((nt, SUBLANES, 2 * S5_TILE_STATES), F32),
        ],
        compiler_params=_cparams("arbitrary", "arbitrary"),
        name="s5_mixer",
    )(proj, bblk, ca, cb, ar, ai, asr, asi, d_skip.reshape(1, width),
      w_glu.astype(BF16), b_glu.reshape(1, width))


def _hgrn_kernel(q_ref, f_ref, i_ref, g_ref, lbp_ref, og_ref, cum_ref, o_ref, st_ref, *, layer):
    @pl.when(pl.program_id(2) == 0)
    def _():
        st_ref[...] = jnp.zeros_like(st_ref)

    for hh in range(st_ref.shape[0]):
        lanes = pl.ds(hh * HGRN_HEAD_DIM, HGRN_HEAD_DIM)
        _hgrn_head(q_ref.at[:, lanes], f_ref.at[:, lanes], i_ref.at[:, lanes], g_ref.at[:, lanes],
                   lbp_ref.at[:, lanes], og_ref, cum_ref, o_ref.at[:, lanes], st_ref.at[hh], layer)


def _hgrn_head(q_ref, f_ref, i_ref, g_ref, lbp_ref, og_ref, cum_ref, o_ref, st_ref, layer):
    n_levels = int(math.log2(HGRN_CHUNK))
    c = HGRN_CHUNK
    hd = HGRN_HEAD_DIM

    lbp = lbp_ref[...]
    e = jnp.exp(lbp - jnp.max(lbp, axis=0, keepdims=True))
    prob = e / jnp.sum(e, axis=0, keepdims=True)
    lb = jnp.sum(prob[:layer + 1], axis=0, keepdims=True) - prob[0:1]
    og = og_ref[...]

    rows = q_ref.shape[0]
    n_chunks = rows // c
    ti = lax.broadcasted_iota(jnp.int32, (c, c), 0)
    si = lax.broadcasted_iota(jnp.int32, (c, c), 1)
    diff_bits = ti ^ si
    tril = cum_ref[...]
    tmod = lax.broadcasted_iota(jnp.int32, (rows, hd), 0) & (c - 1)
    nt_dims = (((1,), (1,)), ((), ()))

    def chunk_rows(x, ch):
        return x[ch * c:(ch + 1) * c]

    forget = lb + (1.0 - lb) * jax.nn.sigmoid(f_ref[...])
    logf = jnp.log2(forget)
    kk = 1.0 - forget
    qq = jax.nn.silu(q_ref[...])
    vb = i_ref[...].astype(BF16)

    hi, lo = _split_bf16(logf)
    b = jnp.concatenate(
        [jnp.dot(tril, chunk_rows(hi, ch), preferred_element_type=F32)
         + jnp.dot(tril, chunk_rows(lo, ch), preferred_element_type=F32) for ch in range(n_chunks)],
        axis=0)

    block_end = b
    scores = [jnp.zeros((c, c), F32) for _ in range(n_chunks)]
    for l in range(n_levels):
        blk = 1 << l
        if l == 0:
            ql = (qq * forget).astype(BF16)
            kl = kk.astype(BF16)
        else:
            prev_end = pltpu.roll(block_end, blk, axis=0)
            eq = b - jnp.where(tmod < blk, 0.0, prev_end)
            ek = block_end - b
            ql = (qq * jnp.exp2(eq)).astype(BF16)
            kl = (kk * jnp.exp2(ek)).astype(BF16)
        mask = ((diff_bits >> l) == 1) & (ti > si)
        for ch in range(n_chunks):
            sl = lax.dot_general(chunk_rows(ql, ch), chunk_rows(kl, ch), nt_dims,
                                 preferred_element_type=F32)
            scores[ch] = jnp.where(mask, sl, scores[ch])
        next_end = pltpu.roll(block_end, rows - blk, axis=0)
        block_end = jnp.where((tmod & blk) == 0, next_end, block_end)

    dsum = jnp.sum(qq * kk, axis=-1, keepdims=True)
    qe = (qq * jnp.exp2(b)).astype(BF16)
    kd = (kk * jnp.exp2(block_end - b)).astype(BF16)
    chunk_decay = jnp.exp2(block_end)

    st = st_ref[...]
    outs = []
    for ch in range(n_chunks):
        sc = jnp.where(ti == si, chunk_rows(dsum, ch), scores[ch]).astype(BF16)
        v_c = chunk_rows(vb, ch)
        outs.append(jnp.dot(sc, v_c, preferred_element_type=F32)
                    + lax.dot_general(chunk_rows(qe, ch), st.astype(BF16), nt_dims,
                                      preferred_element_type=F32))
        st = (st * chunk_decay[ch * c:ch * c + 1]
              + lax.dot_general(v_c, chunk_rows(kd, ch), (((0,), (0,)), ((), ())),
                                preferred_element_type=F32))
    st_ref[...] = st

    out = jnp.concatenate(outs, axis=0)
    ms = jnp.mean(out * out, axis=-1, keepdims=True)
    o_ref[...] = (out * lax.rsqrt(ms + EPS) * og * jax.nn.silu(g_ref[...])).astype(o_ref.dtype)


def _hgrn_tril():
    c = HGRN_CHUNK
    return (jnp.arange(c)[None, :] <= jnp.arange(c)[:, None]).astype(BF16)


def hgrn_mixer(proj, batch, col0, lb_param, o_gain, layer):
    m = proj.shape[0]
    n_layers, width = lb_param.shape
    heads = width // HGRN_HEAD_DIM
    steps = m // batch // HGRN_ROWS
    hd = HGRN_HEAD_DIM
    cum = _hgrn_tril()

    hw = HGRN_STEP_HEADS * hd

    def col(block):
        return pl.BlockSpec((HGRN_ROWS, hw),
                            lambda b, h, n: (b * steps + n, (col0 + block * width) // hw + h))

    return pl.pallas_call(
        functools.partial(_hgrn_kernel, layer=layer),
        grid=(batch, heads // HGRN_STEP_HEADS, steps),
        in_specs=[
            col(0), col(1), col(2), col(3),
            pl.BlockSpec((n_layers, hw), lambda b, h, n: (0, h)),
            pl.BlockSpec((1, hd), lambda b, h, n: (0, 0)),
            pl.BlockSpec(cum.shape, lambda b, h, n: (0, 0)),
        ],
        out_specs=pl.BlockSpec((HGRN_ROWS, hw), lambda b, h, n: (b * steps + n, h)),
        out_shape=jax.ShapeDtypeStruct((m, width), BF16),
        scratch_shapes=[pltpu.VMEM((HGRN_STEP_HEADS, hd, hd), F32)],
        compiler_params=_cparams("arbitrary", "arbitrary", "arbitrary"),
        name="hgrn_mixer",
    )(proj, proj, proj, proj, lb_param, o_gain.reshape(1, hd), cum)


def _attn_kernel(sink_ref, slope_ref, q_ref, kc_ref, vc_ref, kp_ref, vp_ref,
                 qg_ref, kg_ref, ind_ref, indt_ref, o_ref, bias_ref, *, n_kv):
    blk = ATT_BLOCK
    hd = ATT_HEAD_DIM
    nblk = pl.program_id(1)
    ind = ind_ref[...]
    indt = indt_ref[...]

    def head_rms_scale(x):
        w = x.shape[1]
        hi, lo = _split_bf16(x * x)
        ss = (jnp.dot(hi, ind[:w], preferred_element_type=F32)
              + jnp.dot(lo, ind[:w], preferred_element_type=F32))
        rs = lax.rsqrt(ss * (1.0 / hd) + EPS)
        hi, lo = _split_bf16(rs)
        return (jnp.dot(hi, indt[:, :w], preferred_element_type=F32)
                + jnp.dot(lo, indt[:, :w], preferred_element_type=F32))

    @pl.when(nblk <= 1)
    def _():
        t = lax.broadcasted_iota(jnp.int32, (blk, 2 * blk), 0)
        s = lax.broadcasted_iota(jnp.int32, (blk, 2 * blk), 1)
        dist = t + blk - s
        valid = (dist >= 0) & (dist < WINDOW) & ((s >= blk) | (nblk > 0))
        negdist = -dist.astype(F32)
        for h in range(bias_ref.shape[0]):
            tab = jnp.where(valid, (slope_ref[h] * LOG2E) * negdist, -jnp.inf)
            bias_ref[h] = jnp.where(s == 0, sink_ref[h] * LOG2E, tab)

    q = q_ref[...]
    qn = (q * head_rms_scale(q) * qg_ref[...] * (hd ** -0.5 * LOG2E)).astype(BF16)
    k = jnp.concatenate([kp_ref[...], kc_ref[...]], axis=0)
    kn = k * head_rms_scale(k) * kg_ref[...]
    v = jnp.concatenate([vp_ref[...], vc_ref[...]], axis=0)
    lane = lax.broadcasted_iota(jnp.int32, (2 * blk, LANES), 1)
    key_row = lax.broadcasted_iota(jnp.int32, (2 * blk, LANES), 0)
    low = lax.broadcasted_iota(jnp.int32, (blk, LANES), 1) < hd

    tiles = ATT_GROUP // 2
    for kv in range(n_kv):
        tile = kv // 2
        k_t = kn[:, tile * LANES:(tile + 1) * LANES]
        v_t = v[:, tile * LANES:(tile + 1) * LANES]
        own = ((lane < hd) if kv % 2 == 0 else (lane >= hd)) & (key_row > 0)
        k_own = jnp.where(own, k_t, 0.0)
        v_own = jnp.where(own, v_t, 0.0)
        k_half = [None, None]
        v_half = [None, None]
        k_half[kv % 2] = k_own
        v_half[kv % 2] = v_own
        k_half[1 - kv % 2] = pltpu.roll(k_own, hd, axis=1)
        v_half[1 - kv % 2] = pltpu.roll(v_own, hd, axis=1)
        for par in range(2):
            k_half[par] = k_half[par].astype(BF16)
            data_half = (lane < hd) if par == 0 else (lane >= hd)
            v_half[par] = jnp.where(data_half, v_half[par], 1.0).astype(BF16)
        for mt in range(tiles):
            qt = kv * tiles + mt
            q_t = qn[:, qt * LANES:(qt + 1) * LANES]
            pv = []
            for par in range(2):
                sc = lax.dot_general(q_t, k_half[par], (((1,), (1,)), ((), ())),
                                     preferred_element_type=F32) + bias_ref[2 * qt + par]
                mx = jnp.max(sc, axis=-1, keepdims=True)
                p = jnp.exp2(sc - mx).astype(BF16)
                pv.append(jnp.dot(p, v_half[par], preferred_element_type=F32))
            num = jnp.where(low, pv[0], pv[1])
            den = pltpu.roll(jnp.where(low, pv[1], pv[0]), hd, axis=1)
            o_ref[:, qt * LANES:(qt + 1) * LANES] = (num / den).astype(o_ref.dtype)


def attention(qkv, batch, n_heads, n_kv, q_gain, k_gain, sinks):
    m = qkv.shape[0]
    hd = ATT_HEAD_DIM
    qw = n_heads * hd
    kw = n_kv * hd
    steps = m // batch // ATT_BLOCK
    slopes = jnp.exp2(-8.0 * jnp.arange(1, n_heads + 1, dtype=F32) / n_heads)
    head_of_lane = jnp.arange(qw) // hd
    ind = (head_of_lane[:, None] == jnp.arange(LANES)[None, :]).astype(BF16)
    cur = lambda b, n, *_: b * steps + n
    prev = lambda b, n, *_: b * steps + jnp.maximum(n - 1, 0)
    grid_spec = pltpu.PrefetchScalarGridSpec(
        num_scalar_prefetch=2,
        grid=(batch, steps),
        in_specs=[
            pl.BlockSpec((ATT_BLOCK, qw), lambda b, n, *_: (cur(b, n), 0)),
            pl.BlockSpec((ATT_BLOCK, kw), lambda b, n, *_: (cur(b, n), qw // kw)),
            pl.BlockSpec((ATT_BLOCK, kw), lambda b, n, *_: (cur(b, n), qw // kw + 1)),
            pl.BlockSpec((ATT_BLOCK, kw), lambda b, n, *_: (prev(b, n), qw // kw)),
            pl.BlockSpec((ATT_BLOCK, kw), lambda b, n, *_: (prev(b, n), qw // kw + 1)),
            pl.BlockSpec((1, qw), lambda b, n, *_: (0, 0)),
            pl.BlockSpec((1, kw), lambda b, n, *_: (0, 0)),
            pl.BlockSpec(ind.shape, lambda b, n, *_: (0, 0)),
            pl.BlockSpec(ind.shape[::-1], lambda b, n, *_: (0, 0)),
        ],
        out_specs=pl.BlockSpec((ATT_BLOCK, qw), lambda b, n, *_: (cur(b, n), 0)),
        scratch_shapes=[pltpu.VMEM((n_heads, ATT_BLOCK, 2 * ATT_BLOCK), F32)],
    )
    return pl.pallas_call(
        functools.partial(_attn_kernel, n_kv=n_kv),
        grid_spec=grid_spec,
        out_shape=jax.ShapeDtypeStruct((m, qw), BF16),
        compiler_params=_cparams("arbitrary", "arbitrary"),
        name="swa_attention",
    )(sinks, slopes, qkv, qkv, qkv, qkv, qkv,
      jnp.tile(q_gain, n_heads).reshape(1, qw), jnp.tile(k_gain, n_kv).reshape(1, kw),
      ind, ind.T)


def kernel(x, even_norm, even_w_in, s5_lambda_re, s5_lambda_im, s5_log_dt, s5_b_re, s5_b_im,
           s5_c_re, s5_c_im, s5_d, s5_w_glu, s5_b_glu, hgrn_lower_bound, hgrn_o_norm, even_w_out,
           odd_norm, odd_w_qkv, q_norm, k_norm, att_sinks, odd_w_out, mlp_norm, mlp_w_up, mlp_w_down):
    batch, seqlen, d_model = x.shape
    depth = mlp_norm.shape[0]
    s5_width = s5_w_glu.shape[1]
    n_heads = att_sinks.shape[1]
    n_kv = n_heads // ATT_GROUP
    assert seqlen % max(S5_ROWS, HGRN_ROWS, ATT_BLOCK) == 0
    h = x.reshape(batch * seqlen, d_model)
    for layer in range(depth):
        j = layer // 2
        if layer % 2 == 0:
            proj = norm_matmul(h, even_norm[j], even_w_in, j, *PROJ_TILE)
            prep = s5_prep(s5_lambda_re[j], s5_lambda_im[j], s5_log_dt[j], s5_b_re[j], s5_b_im[j])
            params = _s5_layouts(*prep, s5_c_re[j], s5_c_im[j])
            y_a = s5_mixer(proj, batch, params, s5_d[j], s5_w_glu[j], s5_b_glu[j])
            y_b = hgrn_mixer(proj, batch, s5_width, hgrn_lower_bound, hgrn_o_norm[j], j)
            h = proj_residual([y_a, y_b], even_w_out, j, h, *OUT_TILE)
        else:
            qkv = norm_matmul(h, odd_norm[j], odd_w_qkv, j, *PROJ_TILE)
            o = attention(qkv, batch, n_heads, n_kv, q_norm[j], k_norm[j], att_sinks[j])
            h = proj_residual([o], odd_w_out, j, h, *OUT_TILE)
        h = mlp_residual(h, mlp_norm[layer], mlp_w_up, mlp_w_down, layer, *MLP_TILE)
    return h.reshape(batch, seqlen, d_model)
```

```python
import functools
import math

import jax
import jax.numpy as jnp
from jax import lax
from jax.experimental import pallas as pl
from jax.experimental.pallas import tpu as pltpu

F32 = jnp.float32
BF16 = jnp.bfloat16
EPS = 1e-6
LOG2E = math.log2(math.e)

LANES = 128
SUBLANES = 8
VMEM_LIMIT = 56 * 1024 * 1024

S5_GROUP_SIZE = 16
S5_STATE = 64
S5_MIN_DECAY = 1e-4
HGRN_HEAD_DIM = 128
HGRN_CHUNK = 64
ATT_HEAD_DIM = 64
ATT_GROUP = 8
WINDOW = 128
ATT_BLOCK = 128

PROJ_TILE = (2048, 512)
OUT_TILE = (2048, 512)
MLP_TILE = (1024, 512)
NORM_ROWS = 128
S5_ROWS = 512
S5_SEGS = SUBLANES
S5_SEG = S5_ROWS // S5_SEGS
S5_PITCH = S5_SEG + 8
S5_TILE_GROUPS = LANES // S5_GROUP_SIZE
S5_TILE_STATES = S5_TILE_GROUPS * S5_STATE
HGRN_ROWS = 1024
HGRN_STEP_HEADS = 2


def _cparams(*sem):
    return pltpu.CompilerParams(dimension_semantics=sem, vmem_limit_bytes=VMEM_LIMIT)


def _split_bf16(x):
    hi = x.astype(BF16)
    lo = (x - hi.astype(F32)).astype(BF16)
    return hi, lo


def _rmsnorm_rows(h_ref, g_ref, xn_ref):
    rows = h_ref.shape[0]
    g = g_ref[...]

    def body(i, carry):
        r = pl.multiple_of(i * NORM_ROWS, NORM_ROWS)
        x = h_ref[pl.ds(r, NORM_ROWS), :]
        ms = jnp.mean(x * x, axis=-1, keepdims=True)
        xn_ref[pl.ds(r, NORM_ROWS), :] = (x * lax.rsqrt(ms + EPS) * g).astype(BF16)
        return carry

    lax.fori_loop(0, rows // NORM_ROWS, body, 0)


def _prefetched_rows(h_hbm, h_buf, h_sem, consume):
    i = pl.program_id(0)
    j = pl.program_id(1)
    tm = h_buf.shape[0]

    def h_copy(tile):
        return pltpu.make_async_copy(h_hbm.at[pl.ds(tile * tm, tm), :], h_buf, h_sem)

    @pl.when((i == 0) & (j == 0))
    def _():
        h_copy(0).start()

    @pl.when(j == 0)
    def _():
        h_copy(i).wait()
        consume()

    @pl.when((j == 1) & (i + 1 < pl.num_programs(0)))
    def _():
        h_copy(i + 1).start()


W_RING = 3


def _norm_matmul_kernel(h_hbm, g_ref, w_hbm, o_ref, xn_ref, h_buf, h_sem, w_buf, w_sem, *, layer):
    nj = pl.num_programs(1)
    step = pl.program_id(0) * nj + pl.program_id(1)
    n_steps = pl.num_programs(0) * nj
    tn = w_buf.shape[2]

    def w_copy(s):
        col = pl.multiple_of((s % nj) * tn, tn)
        slot = s % W_RING
        return pltpu.make_async_copy(w_hbm.at[layer, :, pl.ds(col, tn)], w_buf.at[slot],
                                     w_sem.at[slot])

    @pl.when(step == 0)
    def _():
        for s in range(W_RING - 1):
            w_copy(s).start()

    @pl.when(step + (W_RING - 1) < n_steps)
    def _():
        w_copy(step + (W_RING - 1)).start()

    _prefetched_rows(h_hbm, h_buf, h_sem, lambda: _rmsnorm_rows(h_buf, g_ref, xn_ref))
    w_copy(step).wait()
    o_ref[...] = jnp.dot(xn_ref[...], w_buf[step % W_RING].astype(BF16),
                         preferred_element_type=F32)


def norm_matmul(h, gain, w_stack, layer, tm, tn):
    m, k = h.shape
    n = w_stack.shape[2]
    tm = min(tm, m)
    assert n // tn >= 2
    assert (m // tm) * (n // tn) >= W_RING - 1
    return pl.pallas_call(
        functools.partial(_norm_matmul_kernel, layer=layer),
        grid=(m // tm, n // tn),
        in_specs=[
            pl.BlockSpec(memory_space=pl.ANY),
            pl.BlockSpec((1, k), lambda i, j: (0, 0)),
            pl.BlockSpec(memory_space=pl.ANY),
        ],
        out_specs=pl.BlockSpec((tm, tn), lambda i, j: (i, j)),
        out_shape=jax.ShapeDtypeStruct((m, n), F32),
        scratch_shapes=[pltpu.VMEM((tm, k), BF16), pltpu.VMEM((tm, k), F32),
                        pltpu.SemaphoreType.DMA(()),
                        pltpu.VMEM((W_RING, k, tn), F32), pltpu.SemaphoreType.DMA((W_RING,))],
        compiler_params=_cparams("arbitrary", "arbitrary"),
        name="norm_matmul",
    )(h, gain.reshape(1, k), w_stack)


def _proj_residual_kernel(*refs, n_lhs):
    lhs_refs = refs[:n_lhs]
    w_ref, res_ref, o_ref = refs[n_lhs:]
    acc = res_ref[...]
    k0 = 0
    for lhs_ref in lhs_refs:
        kk = lhs_ref.shape[1]
        acc = acc + jnp.dot(lhs_ref[...], w_ref[k0:k0 + kk, :].astype(BF16),
                            preferred_element_type=F32)
        k0 += kk
    o_ref[...] = acc


def proj_residual(lhs_list, w_stack, layer, res, tm, tn):
    m, n = res.shape
    k = w_stack.shape[1]
    tm = min(tm, m)
    in_specs = [pl.BlockSpec((tm, x.shape[1]), lambda i, j: (i, 0)) for x in lhs_list]
    in_specs += [pl.BlockSpec((None, k, tn), lambda i, j: (layer, 0, j)),
                 pl.BlockSpec((tm, tn), lambda i, j: (i, j))]
    return pl.pallas_call(
        functools.partial(_proj_residual_kernel, n_lhs=len(lhs_list)),
        grid=(m // tm, n // tn),
        in_specs=in_specs,
        out_specs=pl.BlockSpec((tm, tn), lambda i, j: (i, j)),
        out_shape=jax.ShapeDtypeStruct((m, n), F32),
        compiler_params=_cparams("parallel", "arbitrary"),
        name="proj_residual",
    )(*lhs_list, w_stack, res)


def _mlp_kernel(h_hbm, g_ref, wu_ref, wd_ref, o_ref, xn_ref, h_buf, h_sem):
    def first_hidden_tile():
        _rmsnorm_rows(h_buf, g_ref, xn_ref)
        o_ref[...] = h_buf[...]

    _prefetched_rows(h_hbm, h_buf, h_sem, first_hidden_tile)
    up = jnp.dot(xn_ref[...], wu_ref[...].astype(BF16), preferred_element_type=F32)
    act = jnp.square(jnp.maximum(up, 0.0)).astype(BF16)
    o_ref[...] += jnp.dot(act, wd_ref[...].astype(BF16), preferred_element_type=F32)


def mlp_residual(h, gain, w_up_stack, w_down_stack, layer, tm, tf):
    m, d = h.shape
    f = w_up_stack.shape[2]
    tm = min(tm, m)
    assert f // tf >= 2
    return pl.pallas_call(
        _mlp_kernel,
        grid=(m // tm, f // tf),
        in_specs=[
            pl.BlockSpec(memory_space=pl.ANY),
            pl.BlockSpec((1, d), lambda i, j: (0, 0)),
            pl.BlockSpec((None, d, tf), lambda i, j: (layer, 0, j)),
            pl.BlockSpec((None, tf, d), lambda i, j: (layer, j, 0)),
        ],
        out_specs=pl.BlockSpec((tm, d), lambda i, j: (i, 0)),
        out_shape=jax.ShapeDtypeStruct((m, d), F32),
        scratch_shapes=[pltpu.VMEM((tm, d), BF16), pltpu.VMEM((tm, d), F32),
                        pltpu.SemaphoreType.DMA(())],
        compiler_params=_cparams("arbitrary", "arbitrary"),
        name="mlp_residual",
    )(h, gain.reshape(1, d), w_up_stack, w_down_stack)


def _s5_prep_kernel(lre_ref, lim_ref, ldt_ref, bre_ref, bim_ref,
                    ar_ref, ai_ref, asr_ref, asi_ref, bbr_ref, bbi_ref):
    lr = jnp.minimum(lre_ref[...], -S5_MIN_DECAY)
    li = lim_ref[...]
    dt = jnp.exp(ldt_ref[...])
    mag = jnp.exp(lr * dt)
    ar = mag * jnp.cos(li * dt)
    ai = mag * jnp.sin(li * dt)
    den = lr * lr + li * li
    zr = ((ar - 1.0) * lr + ai * li) / den
    zi = (ai * lr - (ar - 1.0) * li) / den
    ar_ref[...] = ar
    ai_ref[...] = ai
    pr, pi = ar, ai
    for _ in range(int(math.log2(S5_SEG))):
        pr, pi = pr * pr - pi * pi, 2.0 * pr * pi
    asr_ref[...] = pr
    asi_ref[...] = pi
    for c in range(S5_GROUP_SIZE):
        br = bre_ref[c]
        bi = bim_ref[c]
        bbr_ref[c] = zr * br - zi * bi
        bbi_ref[c] = zr * bi + zi * br


def s5_prep(lam_re, lam_im, log_dt, b_re, b_im):
    g, p = lam_re.shape
    ldt = jnp.broadcast_to(log_dt[:, None], (g, p))
    bre_t = jnp.transpose(b_re, (2, 0, 1))
    bim_t = jnp.transpose(b_im, (2, 0, 1))
    gp = jax.ShapeDtypeStruct((g, p), F32)
    cgp = jax.ShapeDtypeStruct((S5_GROUP_SIZE, g, p), F32)
    return pl.pallas_call(
        _s5_prep_kernel,
        out_shape=(gp, gp, gp, gp, cgp, cgp),
        name="s5_prep",
    )(lam_re, lam_im, ldt, bre_t, bim_t)


def _s5_kernel(u_ref, bblk_ref, ca_ref, cb_ref, ar_ref, ai_ref, asr_ref, asi_ref,
               d_ref, wglu_ref, bglu_ref, o_ref,
               bu_ref, xp_ref, yp_ref, yn_ref, st_ref):
    n_tiles = u_ref.shape[1] // LANES
    half = S5_TILE_STATES // LANES
    slabs = 2 * half

    @pl.when(pl.program_id(1) == 0)
    def _():
        st_ref[...] = jnp.zeros_like(st_ref)

    u = u_ref[...]
    ub = u.astype(BF16)
    for j in range(n_tiles):
        res = jnp.dot(ub[:, j * LANES:(j + 1) * LANES], bblk_ref[j], preferred_element_type=F32)
        for seg in range(S5_SEGS):
            for k in range(slabs):
                bu_ref[j * slabs + k, seg * S5_PITCH:seg * S5_PITCH + S5_SEG, :] = (
                    res[seg * S5_SEG:(seg + 1) * S5_SEG, k * LANES:(k + 1) * LANES])

    row = lax.broadcasted_iota(jnp.int32, (SUBLANES, LANES), 0)

    def tile_scan(j, carry):
        a_r = [ar_ref[j, :, k * LANES:(k + 1) * LANES] for k in range(half)]
        a_i = [ai_ref[j, :, k * LANES:(k + 1) * LANES] for k in range(half)]

        def load(s, k):
            return bu_ref[j * slabs + k, pl.ds(s, S5_SEGS, stride=S5_PITCH), :]

        def step1(s, x):
            xr, xi = x
            nr, ni = [], []
            for k in range(half):
                br = load(s, k)
                bi = load(s, half + k)
                nr.append(a_r[k] * xr[k] - a_i[k] * xi[k] + br)
                ni.append(a_r[k] * xi[k] + a_i[k] * xr[k] + bi)
            return tuple(nr), tuple(ni)

        zero = tuple(jnp.zeros((SUBLANES, LANES), F32) for _ in range(half))
        fr, fi = lax.fori_loop(0, S5_SEG, step1, (zero, zero), unroll=8)

        init_r, init_i, new_r, new_i = [], [], [], []
        for k in range(half):
            sr = asr_ref[j, :, k * LANES:(k + 1) * LANES]
            si = asi_ref[j, :, k * LANES:(k + 1) * LANES]
            pr = st_ref[j, :, k * LANES:(k + 1) * LANES]
            pi = st_ref[j, :, (half + k) * LANES:(half + k + 1) * LANES]
            ir = jnp.where(row == 0, pr, 0.0)
            ii = jnp.where(row == 0, pi, 0.0)
            for seg in range(1, S5_SEGS):
                er = fr[k] + sr * ir - si * ii
                ei = fi[k] + sr * ii + si * ir
                ir = jnp.where(row == seg, pltpu.roll(er, 1, axis=0), ir)
                ii = jnp.where(row == seg, pltpu.roll(ei, 1, axis=0), ii)
            er = fr[k] + sr * ir - si * ii
            ei = fi[k] + sr * ii + si * ir
            init_r.append(ir)
            init_i.append(ii)
            new_r.append(pltpu.roll(er, 1, axis=0))
            new_i.append(pltpu.roll(ei, 1, axis=0))
        for k in range(half):
            st_ref[j, :, k * LANES:(k + 1) * LANES] = new_r[k]
            st_ref[j, :, (half + k) * LANES:(half + k + 1) * LANES] = new_i[k]

        def step2(s2, x):
            xr, xn = x
            outs_r, outs_n = [], []
            for _ in range(2):
                outs_r.append([])
                outs_n.append([])
            for t in range(2):
                s = s2 * 2 + t
                nr, nn = [], []
                for k in range(half):
                    br = load(s, k)
                    bi = load(s, half + k)
                    nr.append(a_r[k] * xr[k] + a_i[k] * xn[k] + br)
                    nn.append(a_r[k] * xn[k] - a_i[k] * xr[k] - bi)
                xr, xn = tuple(nr), tuple(nn)
                outs_r[t] = nr
                outs_n[t] = nn
            r0 = pl.multiple_of(s2 * (2 * S5_SEGS), 2 * S5_SEGS)
            for k in range(half):
                xp_ref[j, pl.ds(r0, 2 * S5_SEGS), k * LANES:(k + 1) * LANES] = (
                    jnp.concatenate([outs_r[0][k], outs_r[1][k]], axis=0).astype(BF16))
                xp_ref[j, pl.ds(r0, 2 * S5_SEGS), (half + k) * LANES:(half + k + 1) * LANES] = (
                    jnp.concatenate([outs_n[0][k], outs_n[1][k]], axis=0).astype(BF16))
            return xr, xn

        neg_i = tuple(-v for v in init_i)
        lax.fori_loop(0, S5_SEG // 2, step2, (tuple(init_r), neg_i), unroll=4)
        return carry

    lax.fori_loop(0, n_tiles, tile_scan, 0)

    for jj in range(n_tiles // 2):
        yy = (jnp.dot(xp_ref[2 * jj], ca_ref[jj], preferred_element_type=F32)
              + jnp.dot(xp_ref[2 * jj + 1], cb_ref[jj], preferred_element_type=F32))
        yp_ref[2 * jj] = yy[:, :LANES]
        yp_ref[2 * jj + 1] = yy[:, LANES:]

    for j in range(n_tiles):
        for seg in range(S5_SEGS):
            yn_ref[seg * S5_SEG:(seg + 1) * S5_SEG, j * LANES:(j + 1) * LANES] = (
                yp_ref[j, pl.ds(seg, S5_SEG, stride=S5_SEGS), :])

    y = yn_ref[...] + d_ref[...] * u
    y = jax.nn.gelu(y)
    z = jnp.dot(y.astype(BF16), wglu_ref[...], preferred_element_type=F32) + bglu_ref[...]
    o_ref[...] = (y * jax.nn.sigmoid(z)).astype(o_ref.dtype)


def _s5_layouts(ar, ai, asr, asi, bbr, bbi, c_re, c_im):
    g, p = ar.shape
    tg = S5_TILE_GROUPS
    nt = g // tg
    eye = jnp.eye(tg, dtype=F32)

    def lanes(a):
        return jnp.broadcast_to(a.reshape(nt, 1, tg * p), (nt, SUBLANES, tg * p))

    bb = jnp.stack([bbr, bbi]).reshape(2, S5_GROUP_SIZE, nt, tg, p)
    bb = jnp.transpose(bb, (2, 3, 1, 0, 4))
    bblk = bb[:, :, :, :, None, :] * eye[None, :, None, None, :, None]
    bblk = bblk.reshape(nt, tg * S5_GROUP_SIZE, 2 * tg * p).astype(BF16)

    cc = jnp.stack([c_re, c_im]).reshape(2, nt, tg, S5_GROUP_SIZE, p)
    cc = jnp.transpose(cc, (1, 0, 2, 4, 3))
    cblk = cc[:, :, :, :, None, :] * eye[None, None, :, None, :, None]
    cblk = cblk.reshape(nt, 2 * tg * p, tg * S5_GROUP_SIZE)
    zeros = jnp.zeros_like(cblk[0::2])
    ca = jnp.concatenate([cblk[0::2], zeros], axis=-1).astype(BF16)
    cb = jnp.concatenate([zeros, cblk[1::2]], axis=-1).astype(BF16)
    return bblk, ca, cb, lanes(ar), lanes(ai), lanes(asr), lanes(asi)


def s5_mixer(proj, batch, params, d_skip, w_glu, b_glu):
    bblk, ca, cb, ar, ai, asr, asi = params
    m = proj.shape[0]
    width = w_glu.shape[0]
    nt = width // LANES
    steps = m // batch // S5_ROWS
    slabs = 2 * S5_TILE_STATES // LANES
    const3 = lambda b, n: (0, 0, 0)
    const2 = lambda b, n: (0, 0)
    once = pl.Buffered(1)
    return pl.pallas_call(
        _s5_kernel,
        grid=(batch, steps),
        in_specs=[
            pl.BlockSpec((S5_ROWS, width), lambda b, n: (b * steps + n, 0)),
            pl.BlockSpec(bblk.shape, const3, pipeline_mode=once),
            pl.BlockSpec(ca.shape, const3, pipeline_mode=once),
            pl.BlockSpec(cb.shape, const3, pipeline_mode=once),
            pl.BlockSpec(ar.shape, const3),
            pl.BlockSpec(ai.shape, const3),
            pl.BlockSpec(asr.shape, const3),
            pl.BlockSpec(asi.shape, const3),
            pl.BlockSpec((1, width), const2),
            pl.BlockSpec((width, width), const2, pipeline_mode=once),
            pl.BlockSpec((1, width), const2),
        ],
        out_specs=pl.BlockSpec((S5_ROWS, width), lambda b, n: (b * steps + n, 0)),
        out_shape=jax.ShapeDtypeStruct((m, width), BF16),
        scratch_shapes=[
            pltpu.VMEM((nt * slabs, S5_SEGS * S5_PITCH, LANES), F32),
            pltpu.VMEM((nt, S5_ROWS, 2 * S5_TILE_STATES), BF16),
            pltpu.VMEM((nt, S5_ROWS, LANES), F32),
            pltpu.VMEM((S5_ROWS, width), F32),
            pltpu.VMEM((nt, SUBLANES, 2 * S5_TILE_STATES), F32),
        ],
        compiler_params=_cparams("arbitrary", "arbitrary"),
        name="s5_mixer",
    )(proj, bblk, ca, cb, ar, ai, asr, asi, d_skip.reshape(1, width),
      w_glu.astype(BF16), b_glu.reshape(1, width))


def _hgrn_kernel(q_ref, f_ref, i_ref, g_ref, lbp_ref, og_ref, cum_ref, o_ref, st_ref, *, layer):
    @pl.when(pl.program_id(2) == 0)
    def _():
        st_ref[...] = jnp.zeros_like(st_ref)

    for hh in range(st_ref.shape[0]):
        lanes = pl.ds(hh * HGRN_HEAD_DIM, HGRN_HEAD_DIM)
        _hgrn_head(q_ref.at[:, lanes], f_ref.at[:, lanes], i_ref.at[:, lanes], g_ref.at[:, lanes],
                   lbp_ref.at[:, lanes], og_ref, cum_ref, o_ref.at[:, lanes], st_ref.at[hh], layer)


def _hgrn_head(q_ref, f_ref, i_ref, g_ref, lbp_ref, og_ref, cum_ref, o_ref, st_ref, layer):
    n_levels = int(math.log2(HGRN_CHUNK))
    c = HGRN_CHUNK
    hd = HGRN_HEAD_DIM

    lbp = lbp_ref[...]
    e = jnp.exp(lbp - jnp.max(lbp, axis=0, keepdims=True))
    prob = e / jnp.sum(e, axis=0, keepdims=True)
    lb = jnp.sum(prob[:layer + 1], axis=0, keepdims=True) - prob[0:1]
    og = og_ref[...]

    rows = q_ref.shape[0]
    n_chunks = rows // c
    ti = lax.broadcasted_iota(jnp.int32, (c, c), 0)
    si = lax.broadcasted_iota(jnp.int32, (c, c), 1)
    diff_bits = ti ^ si
    tril = cum_ref[...]
    tmod = lax.broadcasted_iota(jnp.int32, (rows, hd), 0) & (c - 1)
    nt_dims = (((1,), (1,)), ((), ()))

    def chunk_rows(x, ch):
        return x[ch * c:(ch + 1) * c]

    forget = lb + (1.0 - lb) * jax.nn.sigmoid(f_ref[...])
    logf = jnp.log2(forget)
    kk = 1.0 - forget
    qq = jax.nn.silu(q_ref[...])
    vb = i_ref[...].astype(BF16)

    hi, lo = _split_bf16(logf)
    b = jnp.concatenate(
        [jnp.dot(tril, chunk_rows(hi, ch), preferred_element_type=F32)
         + jnp.dot(tril, chunk_rows(lo, ch), preferred_element_type=F32) for ch in range(n_chunks)],
        axis=0)

    block_end = b
    scores = [jnp.zeros((c, c), F32) for _ in range(n_chunks)]
    for l in range(n_levels):
        blk = 1 << l
        if l == 0:
            ql = (qq * forget).astype(BF16)
            kl = kk.astype(BF16)
        else:
            prev_end = pltpu.roll(block_end, blk, axis=0)
            eq = b - jnp.where(tmod < blk, 0.0, prev_end)
            ek = block_end - b
            ql = (qq * jnp.exp2(eq)).astype(BF16)
            kl = (kk * jnp.exp2(ek)).astype(BF16)
        mask = ((diff_bits >> l) == 1) & (ti > si)
        for ch in range(n_chunks):
            sl = lax.dot_general(chunk_rows(ql, ch), chunk_rows(kl, ch), nt_dims,
                                 preferred_element_type=F32)
            scores[ch] = jnp.where(mask, sl, scores[ch])
        next_end = pltpu.roll(block_end, rows - blk, axis=0)
        block_end = jnp.where((tmod & blk) == 0, next_end, block_end)

    dsum = jnp.sum(qq * kk, axis=-1, keepdims=True)
    qe = (qq * jnp.exp2(b)).astype(BF16)
    kd = (kk * jnp.exp2(block_end - b)).astype(BF16)
    chunk_decay = jnp.exp2(block_end)

    st = st_ref[...]
    outs = []
    for ch in range(n_chunks):
        sc = jnp.where(ti == si, chunk_rows(dsum, ch), scores[ch]).astype(BF16)
        v_c = chunk_rows(vb, ch)
        outs.append(jnp.dot(sc, v_c, preferred_element_type=F32)
                    + lax.dot_general(chunk_rows(qe, ch), st.astype(BF16), nt_dims,
                                      preferred_element_type=F32))
        st = (st * chunk_decay[ch * c:ch * c + 1]
              + lax.dot_general(v_c, chunk_rows(kd, ch), (((0,), (0,)), ((), ())),
                                preferred_element_type=F32))
    st_ref[...] = st

    out = jnp.concatenate(outs, axis=0)
    ms = jnp.mean(out * out, axis=-1, keepdims=True)
    o_ref[...] = (out * lax.rsqrt(ms + EPS) * og * jax.nn.silu(g_ref[...])).astype(o_ref.dtype)


def _hgrn_tril():
    c = HGRN_CHUNK
    return (jnp.arange(c)[None, :] <= jnp.arange(c)[:, None]).astype(BF16)


def hgrn_mixer(proj, batch, col0, lb_param, o_gain, layer):
    m = proj.shape[0]
    n_layers, width = lb_param.shape
    heads = width // HGRN_HEAD_DIM
    steps = m // batch // HGRN_ROWS
    hd = HGRN_HEAD_DIM
    cum = _hgrn_tril()

    hw = HGRN_STEP_HEADS * hd

    def col(block):
        return pl.BlockSpec((HGRN_ROWS, hw),
                            lambda b, h, n: (b * steps + n, (col0 + block * width) // hw + h))

    return pl.pallas_call(
        functools.partial(_hgrn_kernel, layer=layer),
        grid=(batch, heads // HGRN_STEP_HEADS, steps),
        in_specs=[
            col(0), col(1), col(2), col(3),
            pl.BlockSpec((n_layers, hw), lambda b, h, n: (0, h)),
            pl.BlockSpec((1, hd), lambda b, h, n: (0, 0)),
            pl.BlockSpec(cum.shape, lambda b, h, n: (0, 0)),
        ],
        out_specs=pl.BlockSpec((HGRN_ROWS, hw), lambda b, h, n: (b * steps + n, h)),
        out_shape=jax.ShapeDtypeStruct((m, width), BF16),
        scratch_shapes=[pltpu.VMEM((HGRN_STEP_HEADS, hd, hd), F32)],
        compiler_params=_cparams("arbitrary", "arbitrary", "arbitrary"),
        name="hgrn_mixer",
    )(proj, proj, proj, proj, lb_param, o_gain.reshape(1, hd), cum)


def _attn_kernel(sink_ref, slope_ref, q_ref, kc_ref, vc_ref, kp_ref, vp_ref,
                 qg_ref, kg_ref, ind_ref, indt_ref, o_ref, bias_ref, *, n_kv):
    blk = ATT_BLOCK
    hd = ATT_HEAD_DIM
    nblk = pl.program_id(1)
    ind = ind_ref[...]
    indt = indt_ref[...]

    def head_rms_scale(x):
        w = x.shape[1]
        hi, lo = _split_bf16(x * x)
        ss = (jnp.dot(hi, ind[:w], preferred_element_type=F32)
              + jnp.dot(lo, ind[:w], preferred_element_type=F32))
        rs = lax.rsqrt(ss * (1.0 / hd) + EPS)
        hi, lo = _split_bf16(rs)
        return (jnp.dot(hi, indt[:, :w], preferred_element_type=F32)
                + jnp.dot(lo, indt[:, :w], preferred_element_type=F32))

    @pl.when(nblk <= 1)
    def _():
        t = lax.broadcasted_iota(jnp.int32, (blk, 2 * blk), 0)
        s = lax.broadcasted_iota(jnp.int32, (blk, 2 * blk), 1)
        dist = t + blk - s
        valid = (dist >= 0) & (dist < WINDOW) & ((s >= blk) | (nblk > 0))
        negdist = -dist.astype(F32)
        for h in range(bias_ref.shape[0]):
            tab = jnp.where(valid, (slope_ref[h] * LOG2E) * negdist, -jnp.inf)
            bias_ref[h] = jnp.where(s == 0, sink_ref[h] * LOG2E, tab)

    q = q_ref[...]
    qn = (q * head_rms_scale(q) * qg_ref[...] * (hd ** -0.5 * LOG2E)).astype(BF16)
    k = jnp.concatenate([kp_ref[...], kc_ref[...]], axis=0)
    kn = k * head_rms_scale(k) * kg_ref[...]
    v = jnp.concatenate([vp_ref[...], vc_ref[...]], axis=0)
    lane = lax.broadcasted_iota(jnp.int32, (2 * blk, LANES), 1)
    key_row = lax.broadcasted_iota(jnp.int32, (2 * blk, LANES), 0)
    low = lax.broadcasted_iota(jnp.int32, (blk, LANES), 1) < hd

    tiles = ATT_GROUP // 2
    for kv in range(n_kv):
        tile = kv // 2
        k_t = kn[:, tile * LANES:(tile + 1) * LANES]
        v_t = v[:, tile * LANES:(tile + 1) * LANES]
        own = ((lane < hd) if kv % 2 == 0 else (lane >= hd)) & (key_row > 0)
        k_own = jnp.where(own, k_t, 0.0)
        v_own = jnp.where(own, v_t, 0.0)
        k_half = [None, None]
        v_half = [None, None]
        k_half[kv % 2] = k_own
        v_half[kv % 2] = v_own
        k_half[1 - kv % 2] = pltpu.roll(k_own, hd, axis=1)
        v_half[1 - kv % 2] = pltpu.roll(v_own, hd, axis=1)
        for par in range(2):
            k_half[par] = k_half[par].astype(BF16)
            data_half = (lane < hd) if par == 0 else (lane >= hd)
            v_half[par] = jnp.where(data_half, v_half[par], 1.0).astype(BF16)
        for mt in range(tiles):
            qt = kv * tiles + mt
            q_t = qn[:, qt * LANES:(qt + 1) * LANES]
            pv = []
            for par in range(2):
                sc = lax.dot_general(q_t, k_half[par], (((1,), (1,)), ((), ())),
                                     preferred_element_type=F32) + bias_ref[2 * qt + par]
                mx = jnp.max(sc, axis=-1, keepdims=True)
                p = jnp.exp2(sc - mx).astype(BF16)
                pv.append(jnp.dot(p, v_half[par], preferred_element_type=F32))
            num = jnp.where(low, pv[0], pv[1])
            den = pltpu.roll(jnp.where(low, pv[1], pv[0]), hd, axis=1)
            o_ref[:, qt * LANES:(qt + 1) * LANES] = (num / den).astype(o_ref.dtype)


def attention(qkv, batch, n_heads, n_kv, q_gain, k_gain, sinks):
    m = qkv.shape[0]
    hd = ATT_HEAD_DIM
    qw = n_heads * hd
    kw = n_kv * hd
    steps = m // batch // ATT_BLOCK
    slopes = jnp.exp2(-8.0 * jnp.arange(1, n_heads + 1, dtype=F32) / n_heads)
    head_of_lane = jnp.arange(qw) // hd
    ind = (head_of_lane[:, None] == jnp.arange(LANES)[None, :]).astype(BF16)
    cur = lambda b, n, *_: b * steps + n
    prev = lambda b, n, *_: b * steps + jnp.maximum(n - 1, 0)
    grid_spec = pltpu.PrefetchScalarGridSpec(
        num_scalar_prefetch=2,
        grid=(batch, steps),
        in_specs=[
            pl.BlockSpec((ATT_BLOCK, qw), lambda b, n, *_: (cur(b, n), 0)),
            pl.BlockSpec((ATT_BLOCK, kw), lambda b, n, *_: (cur(b, n), qw // kw)),
            pl.BlockSpec((ATT_BLOCK, kw), lambda b, n, *_: (cur(b, n), qw // kw + 1)),
            pl.BlockSpec((ATT_BLOCK, kw), lambda b, n, *_: (prev(b, n), qw // kw)),
            pl.BlockSpec((ATT_BLOCK, kw), lambda b, n, *_: (prev(b, n), qw // kw + 1)),
            pl.BlockSpec((1, qw), lambda b, n, *_: (0, 0)),
            pl.BlockSpec((1, kw), lambda b, n, *_: (0, 0)),
            pl.BlockSpec(ind.shape, lambda b, n, *_: (0, 0)),
            pl.BlockSpec(ind.shape[::-1], lambda b, n, *_: (0, 0)),
        ],
        out_specs=pl.BlockSpec((ATT_BLOCK, qw), lambda b, n, *_: (cur(b, n), 0)),
        scratch_shapes=[pltpu.VMEM((n_heads, ATT_BLOCK, 2 * ATT_BLOCK), F32)],
    )
    return pl.pallas_call(
        functools.partial(_attn_kernel, n_kv=n_kv),
        grid_spec=grid_spec,
        out_shape=jax.ShapeDtypeStruct((m, qw), BF16),
        compiler_params=_cparams("arbitrary", "arbitrary"),
        name="swa_attention",
    )(sinks, slopes, qkv, qkv, qkv, qkv, qkv,
      jnp.tile(q_gain, n_heads).reshape(1, qw), jnp.tile(k_gain, n_kv).reshape(1, kw),
      ind, ind.T)


def kernel(x, even_norm, even_w_in, s5_lambda_re, s5_lambda_im, s5_log_dt, s5_b_re, s5_b_im,
           s5_c_re, s5_c_im, s5_d, s5_w_glu, s5_b_glu, hgrn_lower_bound, hgrn_o_norm, even_w_out,
           odd_norm, odd_w_qkv, q_norm, k_norm, att_sinks, odd_w_out, mlp_norm, mlp_w_up, mlp_w_down):
    batch, seqlen, d_model = x.shape
    depth = mlp_norm.shape[0]
    s5_width = s5_w_glu.shape[1]
    n_heads = att_sinks.shape[1]
    n_kv = n_heads // ATT_GROUP
    assert seqlen % max(S5_ROWS, HGRN_ROWS, ATT_BLOCK) == 0
    h = x.reshape(batch * seqlen, d_model)
    for layer in range(depth):
        j = layer // 2
        if layer % 2 == 0:
            proj = norm_matmul(h, even_norm[j], even_w_in, j, *PROJ_TILE)
            prep = s5_prep(s5_lambda_re[j], s5_lambda_im[j], s5_log_dt[j], s5_b_re[j], s5_b_im[j])
            params = _s5_layouts(*prep, s5_c_re[j], s5_c_im[j])
            y_a = s5_mixer(proj, batch, params, s5_d[j], s5_w_glu[j], s5_b_glu[j])
            y_b = hgrn_mixer(proj, batch, s5_width, hgrn_lower_bound, hgrn_o_norm[j], j)
            h = proj_residual([y_a, y_b], even_w_out, j, h, *OUT_TILE)
        else:
            qkv = norm_matmul(h, odd_norm[j], odd_w_qkv, j, *PROJ_TILE)
            o = attention(qkv, batch, n_heads, n_kv, q_norm[j], k_norm[j], att_sinks[j])
            h = proj_residual([o], odd_w_out, j, h, *OUT_TILE)
        h = mlp_residual(h, mlp_norm[layer], mlp_w_up, mlp_w_down, layer, *MLP_TILE)
    return h.reshape(batch, seqlen, d_model)
```
